```python
import math
import jax, jax.numpy as jnp
from jax import lax
import numpy as np

D_MODEL = 2048
BATCH = 2
SEQ = 8192
DEPTH = 4

N_MEM = 256
MIX_WIDTH = D_MODEL
ATTN_WIDTH = MIX_WIDTH // 2
HG_WIDTH = MIX_WIDTH - ATTN_WIDTH
ATTN_HEAD_DIM = 128
ATTN_HEADS = ATTN_WIDTH // ATTN_HEAD_DIM
HG_EXPAND = 128
HG_HEADS = HG_WIDTH // HG_EXPAND
HG_VDIM = HG_WIDTH // HG_HEADS
HG_CHUNK = 64
DILATED_BRANCHES = ((128, 1), (512, 4), (2048, 16))
Q_BLOCK = 128
REL_BUCKETS = 32
REL_MAX_DIST = 2048
CROSS_HEADS = 4
CROSS_HEAD_DIM = D_MODEL // CROSS_HEADS
D_FF = ((8 * D_MODEL + 3 * 256 - 1) // (3 * 256)) * 256
IN_SIZES = (ATTN_WIDTH, ATTN_WIDTH, ATTN_WIDTH, HG_WIDTH, HG_WIDTH, HG_WIDTH, HG_WIDTH)
IN_COLS = sum(IN_SIZES)
N_NORMS = 7
RMS_EPS = 1e-6
NEG_INF = -1e30

kernel_name = "hybrid_dilated_attn_hgrn2_trunk"


def rmsnorm(x, g):
    xf = x.astype(jnp.float32)
    y = xf * lax.rsqrt(jnp.mean(xf * xf, axis=-1, keepdims=True) + RMS_EPS)
    return (y * g.astype(jnp.float32)).astype(x.dtype)


def rel_bucket(dist):
    max_exact = REL_BUCKETS // 2
    d_f = jnp.maximum(dist, 1).astype(jnp.float32)
    large = max_exact + (jnp.log(d_f / max_exact) / math.log(REL_MAX_DIST / max_exact)
                         * (REL_BUCKETS - max_exact)).astype(jnp.int32)
    large = jnp.minimum(large, REL_BUCKETS - 1)
    return jnp.where(dist < max_exact, dist, large)


def dilated_branch(q, k, v, rel_bias, window, dilation):
    B, S, H, Dh = q.shape
    band = window // dilation
    assert band <= Q_BLOCK
    M = S // dilation
    Mp = -(-M // Q_BLOCK) * Q_BLOCK
    nb = Mp // Q_BLOCK

    def split(t):
        t = t.reshape(B, M, dilation, H, Dh)
        return jnp.pad(t, ((0, 0), (0, Mp - M), (0, 0), (0, 0), (0, 0)))

    def kwin(t):
        tp = jnp.pad(split(t), ((0, 0), (Q_BLOCK, 0), (0, 0), (0, 0), (0, 0)))
        tp = tp.reshape(B, nb + 1, Q_BLOCK, dilation, H, Dh)
        return jnp.concatenate([tp[:, :-1], tp[:, 1:]], axis=2)

    qs = split(q).reshape(B, nb, Q_BLOCK, dilation, H, Dh)
    ks, vs = kwin(k), kwin(v)
    scores = jnp.einsum('bnqrhd,bnkrhd->bnrhqk', qs, ks).astype(jnp.float32) * (Dh ** -0.5)

    qi = jnp.arange(Q_BLOCK)[:, None]
    kj = jnp.arange(2 * Q_BLOCK)[None, :]
    m = qi - kj + Q_BLOCK
    bucket = rel_bucket(jnp.maximum(m, 0) * dilation)
    bias = jnp.transpose(rel_bias[bucket].astype(jnp.float32), (2, 0, 1))
    band_ok = (m >= 0) & (m <= band)
    blk = jnp.arange(nb)[:, None, None]
    start_ok = (blk * Q_BLOCK + kj[None] - Q_BLOCK) >= 0
    valid = band_ok[None] & start_ok
    logits = jnp.where(valid[None, :, None, None], scores + bias, NEG_INF)
    lse = jax.nn.logsumexp(logits, axis=-1)
    p = jnp.exp(logits - lse[..., None])
    o = jnp.einsum('bnrhqk,bnkrhd->bnqrhd', p.astype(vs.dtype), vs)
    o = o.reshape(B, Mp, dilation, H, Dh)[:, :M].reshape(B, S, H, Dh)
    lse = jnp.transpose(lse, (0, 1, 4, 2, 3)).reshape(B, Mp, dilation, H)[:, :M].reshape(B, S, H)
    return o, lse


def dilated_attention(q, k, v, rel_bias):
    outs, lses = [], []
    for window, dilation in DILATED_BRANCHES:
        o, l = dilated_branch(q, k, v, rel_bias, window, dilation)
        outs.append(o)
        lses.append(l)
    w = jax.nn.softmax(jnp.stack(lses), axis=0)
    o = jnp.einsum('gbsh,gbshd->bshd', w.astype(q.dtype), jnp.stack(outs))
    return o


def hgrn2(fz, iv, qz, gz, lb, hg_gain):
    B, S, _ = fz.shape
    nc = S // HG_CHUNK
    f = lb + (1.0 - lb) * jax.nn.sigmoid(fz.astype(jnp.float32))
    log_f = jnp.log(f)
    kk = 1.0 - f
    qq = jax.nn.silu(qz.astype(jnp.float32))
    vv = iv.astype(jnp.float32)

    def chunks(t, dim):
        return jnp.transpose(t.reshape(B, nc, HG_CHUNK, HG_HEADS, dim), (1, 0, 3, 2, 4))

    xs = (chunks(qq, HG_EXPAND), chunks(kk, HG_EXPAND), chunks(vv, HG_VDIM), chunks(log_f, HG_EXPAND))
    tri = jnp.tril(jnp.ones((HG_CHUNK, HG_CHUNK), dtype=bool))

    def step(state, inp):
        qc, kc, vc, lfc = inp
        b = jnp.cumsum(lfc, axis=-2)
        o_inter = jnp.einsum('bhck,bhkv->bhcv', qc * jnp.exp(b), state)
        diff = b[:, :, :, None, :] - b[:, :, None, :, :]
        decay = jnp.exp(jnp.where(tri[:, :, None], diff, -jnp.inf))
        A = jnp.einsum('bhtk,bhtsk,bhsk->bhts', qc, decay, kc)
        o = o_inter + jnp.einsum('bhts,bhsv->bhtv', A, vc)
        b_last = b[:, :, -1:, :]
        new_state = jnp.exp(b_last[:, :, 0, :])[..., None] * state + \
            jnp.einsum('bhsk,bhsv->bhkv', kc * jnp.exp(b_last - b), vc)
        return new_state, o

    s0 = jnp.zeros((B, HG_HEADS, HG_EXPAND, HG_VDIM), jnp.float32)
    _, o = lax.scan(step, s0, xs)
    o = jnp.transpose(o, (1, 0, 3, 2, 4)).reshape(B, S, HG_HEADS, HG_VDIM)
    o = o * lax.rsqrt(jnp.mean(o * o, axis=-1, keepdims=True) + RMS_EPS)
    o = o * hg_gain.astype(jnp.float32).reshape(HG_HEADS, HG_VDIM)
    o = o.reshape(B, S, HG_WIDTH) * jax.nn.silu(gz.astype(jnp.float32))
    return o.astype(fz.dtype)


def setup_inputs(seed: int = 0) -> dict:
    key = jax.random.key(seed)
    ks = jax.random.split(key, 14)
    f32 = jnp.float32
    nrm = lambda k, shape, scale: jax.random.normal(k, shape, f32) * scale
    return {
        "x": nrm(ks[0], (BATCH, SEQ, D_MODEL), 1.0),
        "mem": nrm(ks[1], (BATCH, N_MEM, D_MODEL), 1.0),
        "rel_bias": nrm(ks[2], (REL_BUCKETS, ATTN_HEADS), 0.5),
        "lb_logits": nrm(ks[3], (DEPTH, HG_WIDTH), 0.5),
        "norm_gains": 1.0 + nrm(ks[4], (DEPTH, N_NORMS, D_MODEL), 0.1),
        "w_in": nrm(ks[5], (DEPTH, D_MODEL, IN_COLS), D_MODEL ** -0.5),
        "hg_norm": 1.0 + nrm(ks[6], (DEPTH, HG_WIDTH), 0.1),
        "w_out": nrm(ks[7], (DEPTH, MIX_WIDTH, D_MODEL), MIX_WIDTH ** -0.5),
        "w_cq": nrm(ks[8], (DEPTH, D_MODEL, D_MODEL), D_MODEL ** -0.5),
        "w_ckv": nrm(ks[9], (DEPTH, D_MODEL, 2 * D_MODEL), D_MODEL ** -0.5),
        "w_co": nrm(ks[10], (DEPTH, D_MODEL, D_MODEL), D_MODEL ** -0.5),
        "w_gate_up": nrm(ks[11], (DEPTH, D_MODEL, 2 * D_FF), D_MODEL ** -0.5),
        "w_down": nrm(ks[12], (DEPTH, D_FF, D_MODEL), D_FF ** -0.5),
    }


def reference(x, mem, rel_bias, lb_logits, norm_gains, w_in, hg_norm, w_out,
              w_cq, w_ckv, w_co, w_gate_up, w_down):
    B, S, D = x.shape
    n_mem = mem.shape[1]
    P = jax.nn.softmax(lb_logits.astype(jnp.float32), axis=0)
    lb_all = jnp.cumsum(P, axis=0) - P
    split_idx = [int(i) for i in np.cumsum(IN_SIZES)[:-1]]

    for l in range(DEPTH):
        g = norm_gains[l]
        h = rmsnorm(x, g[0])
        proj = h @ w_in[l]
        aq, ak, av, fz, iv, qz, gz = jnp.split(proj, split_idx, axis=-1)
        shp = (B, S, ATTN_HEADS, ATTN_HEAD_DIM)
        attn = dilated_attention(aq.reshape(shp), ak.reshape(shp), av.reshape(shp), rel_bias)
        attn = attn.reshape(B, S, ATTN_WIDTH)
        hg = hgrn2(fz, iv, qz, gz, lb_all[l], hg_norm[l])
        mix = jnp.concatenate([attn, hg], axis=-1) @ w_out[l]
        x = x + rmsnorm(mix, g[1])
        hc = rmsnorm(x, g[2])
        mn = rmsnorm(mem, g[3])
        cq = (hc @ w_cq[l]).reshape(B, S, CROSS_HEADS, CROSS_HEAD_DIM)
        ck, cv = jnp.split(mn @ w_ckv[l], 2, axis=-1)
        ck = ck.reshape(B, n_mem, CROSS_HEADS, CROSS_HEAD_DIM)
        cv = cv.reshape(B, n_mem, CROSS_HEADS, CROSS_HEAD_DIM)
        s = jnp.einsum('bshd,bmhd->bhsm', cq, ck).astype(jnp.float32) * (CROSS_HEAD_DIM ** -0.5)
        p = jax.nn.softmax(s, axis=-1).astype(cv.dtype)
        co = jnp.einsum('bhsm,bmhd->bshd', p, cv).reshape(B, S, D) @ w_co[l]
        x = x + rmsnorm(co, g[4])
        hf = rmsnorm(x, g[5])
        gate, up = jnp.split(hf @ w_gate_up[l], 2, axis=-1)
        y = (jax.nn.silu(gate) * up) @ w_down[l]
        x = x + rmsnorm(y, g[6])
    return x
```

```python
import functools
import math

import numpy as np
import jax
import jax.numpy as jnp
from jax import lax
from jax.experimental import pallas as pl
from jax.experimental.pallas import tpu as pltpu

F32 = jnp.float32
BF16 = jnp.bfloat16

HEAD_DIM = 128
ATTN_HEADS = 8
HG_HEADS = 8
ATTN_WIDTH = ATTN_HEADS * HEAD_DIM
HG_WIDTH = HG_HEADS * HEAD_DIM
Q_BLOCK = 128
DILATED_BRANCHES = ((128, 1), (512, 4), (2048, 16))
REL_BUCKETS = 32
REL_MAX_DIST = 2048
CROSS_HEADS = 4
RMS_EPS = 1e-6
NEG_INF = -1e30
HG_CHUNK = 128
HG_LEVELS = 7
VMEM_LIMIT = 52 * 1024 * 1024


def _rms(x, g):
    return x * lax.rsqrt(jnp.mean(x * x, axis=-1, keepdims=True) + RMS_EPS) * g


def _sigmoid(x):
    return 1.0 / (1.0 + jnp.exp(-x))


def _dot(a, b):
    return jnp.dot(a, b, preferred_element_type=F32)


def _dot_nt(a, b):
    return lax.dot_general(a, b, (((1,), (1,)), ((), ())), preferred_element_type=F32)


def _dot_tn(a, b):
    return lax.dot_general(a, b, (((0,), (0,)), ((), ())), preferred_element_type=F32)


def _params(sem):
    return pltpu.CompilerParams(dimension_semantics=sem, vmem_limit_bytes=VMEM_LIMIT)


def _const_spec(shape):
    nd = len(shape)
    return pl.BlockSpec(shape, lambda *_: (0,) * nd, pipeline_mode=pl.Buffered(1))


def _lb_kernel(z_ref, o_ref):
    z = z_ref[...]
    e = jnp.exp(z - jnp.max(z, axis=0, keepdims=True))
    p = e / jnp.sum(e, axis=0, keepdims=True)
    acc = jnp.zeros_like(p[0:1])
    for l in range(z.shape[0]):
        o_ref[l:l + 1, :] = acc
        acc = acc + p[l:l + 1]


def _lower_bounds(lb_logits):
    return pl.pallas_call(
        _lb_kernel, out_shape=jax.ShapeDtypeStruct(lb_logits.shape, F32))(lb_logits.astype(F32))


def _norm_matmul_kernel(x_ref, g_ref, w_ref, o_ref, h_scr):
    @pl.when(pl.program_id(1) == 0)
    def _():
        h_scr[...] = _rms(x_ref[...], g_ref[...]).astype(BF16)

    o_ref[...] = _dot(h_scr[...], w_ref[...]).astype(o_ref.dtype)


def _norm_matmul(x, g, w, out_dtype, tm, tn):
    m, d = x.shape
    n = w.shape[1]
    return pl.pallas_call(
        _norm_matmul_kernel,
        grid=(m // tm, n // tn),
        in_specs=[pl.BlockSpec((tm, d), lambda i, j: (i, 0)),
                  pl.BlockSpec((1, d), lambda i, j: (0, 0)),
                  pl.BlockSpec((d, tn), lambda i, j: (0, j))],
        out_specs=pl.BlockSpec((tm, tn), lambda i, j: (i, j)),
        out_shape=jax.ShapeDtypeStruct((m, n), out_dtype),
        scratch_shapes=[pltpu.VMEM((tm, d), BF16)],
        compiler_params=_params(("parallel", "arbitrary")),
    )(x, g, w)


def _bucket_table(dilation, band):
    qi = np.arange(Q_BLOCK)[:, None]
    kj = np.arange(2 * Q_BLOCK)[None, :]
    m = qi - kj + Q_BLOCK
    dist = np.maximum(m, 0) * dilation
    max_exact = REL_BUCKETS // 2
    d_f = np.maximum(dist, 1).astype(np.float32)
    large = max_exact + (np.log(d_f / np.float32(max_exact)) / np.float32(math.log(REL_MAX_DIST / max_exact))
                         * (REL_BUCKETS - max_exact)).astype(np.int32)
    large = np.minimum(large, REL_BUCKETS - 1)
    bucket = np.where(dist < max_exact, dist, large)
    return np.where((m >= 0) & (m <= band), bucket, -1).astype(np.int32)


def _dilated_kernel(rb_ref, bucket_ref, q_ref, kp_ref, kc_ref, vp_ref, vc_ref, o_ref, lse_ref, bias_scr):
    first = (pl.program_id(0) == 0) & (pl.program_id(1) == 0) & (pl.program_id(2) == 0)

    @pl.when(first)
    def _():
        bucket = bucket_ref[...]
        for h in range(ATTN_HEADS):
            acc = jnp.full(bucket.shape, NEG_INF, F32)
            for b in range(REL_BUCKETS):
                acc = jnp.where(bucket == b, rb_ref[b, h], acc)
            bias_scr[h] = acc

    col = lax.broadcasted_iota(jnp.int32, (Q_BLOCK, 2 * Q_BLOCK), 1)
    before_start = (pl.program_id(1) == 0) & (col < Q_BLOCK)
    lane = lax.broadcasted_iota(jnp.int32, (Q_BLOCK, HEAD_DIM), 1)
    scale = HEAD_DIM ** -0.5
    lse_all = jnp.zeros((Q_BLOCK, HEAD_DIM), F32)
    for h in range(ATTN_HEADS):
        sl = slice(h * HEAD_DIM, (h + 1) * HEAD_DIM)
        q = q_ref[:, sl]
        k = jnp.concatenate([kp_ref[:, sl], kc_ref[:, sl]], axis=0)
        v = jnp.concatenate([vp_ref[:, sl], vc_ref[:, sl]], axis=0)
        bias = bias_scr[h]
        s = jnp.where(bias > 0.5 * NEG_INF, _dot_nt(q, k) * scale + bias, NEG_INF)
        s = jnp.where(before_start, NEG_INF, s)
        m = jnp.max(s, axis=-1, keepdims=True)
        p = jnp.exp(s - m)
        l = jnp.sum(p, axis=-1, keepdims=True)
        o = _dot(p.astype(BF16), v) / l
        o_ref[:, sl] = o.astype(o_ref.dtype)
        lse_all = jnp.where(lane == h, m + jnp.log(l), lse_all)
    lse_ref[...] = lse_all


def _dilated_branch(qkv, rel_bias, batch, seq, window, dilation):
    band = window // dilation
    msub = seq // dilation
    nb = msub // Q_BLOCK
    w = ATTN_WIDTH
    qkv_v = qkv.reshape(batch, msub, dilation * 3 * w)
    bucket = jnp.asarray(_bucket_table(dilation, band))
    blk = (None, Q_BLOCK, w)
    o, lse = pl.pallas_call(
        _dilated_kernel,
        grid=(batch, nb, dilation),
        in_specs=[pl.BlockSpec(memory_space=pltpu.SMEM),
                  pl.BlockSpec((Q_BLOCK, 2 * Q_BLOCK), lambda b, n, r: (0, 0)),
                  pl.BlockSpec(blk, lambda b, n, r: (b, n, 3 * r)),
                  pl.BlockSpec(blk, lambda b, n, r: (b, jnp.maximum(n - 1, 0), 3 * r + 1)),
                  pl.BlockSpec(blk, lambda b, n, r: (b, n, 3 * r + 1)),
                  pl.BlockSpec(blk, lambda b, n, r: (b, jnp.maximum(n - 1, 0), 3 * r + 2)),
                  pl.BlockSpec(blk, lambda b, n, r: (b, n, 3 * r + 2))],
        out_specs=[pl.BlockSpec(blk, lambda b, n, r: (b, n, r)),
                   pl.BlockSpec((None, Q_BLOCK, HEAD_DIM), lambda b, n, r: (b, n, r))],
        out_shape=[jax.ShapeDtypeStruct((batch, msub, dilation * w), BF16),
                   jax.ShapeDtypeStruct((batch, msub, dilation * HEAD_DIM), F32)],
        scratch_shapes=[pltpu.VMEM((ATTN_HEADS, Q_BLOCK, 2 * Q_BLOCK), F32)],
        compiler_params=_params(("arbitrary", "arbitrary", "arbitrary")),
    )(rel_bias, bucket, qkv_v, qkv_v, qkv_v, qkv_v, qkv_v)
    return o.reshape(batch * seq, w), lse.reshape(batch * seq, HEAD_DIM)


def _hg_tables():
    c = HG_CHUNK
    t = np.arange(c)[:, None]
    u = np.arange(c)[None, :]
    mats = []
    for j in range(1, HG_LEVELS + 1):
        s = 1 << j
        if s < c:
            upper = (t & s) != 0
        else:
            upper = np.ones_like(t, dtype=bool)
        start = t & ~(s - 1)
        end = t | (s - 1)
        q_side = (u >= start) & (u <= t)
        k_side = (u > t) & (u <= end)
        if s < c:
            mats.append(np.where(upper, q_side, k_side))
        else:
            mats.append(u <= t)
            mats.append(u > t)
    stack = np.concatenate(mats, axis=0).astype(np.float32)
    x = t ^ u
    level = np.where(u < t, np.floor(np.log2(np.maximum(x, 1))).astype(np.int32), -1)
    level = np.where(u == t, HG_LEVELS, level).astype(np.int32)
    return stack, level


def _hgrn_kernel(stack_ref, level_ref, lb_ref, gain_ref, fz_ref, iv_ref, qz_ref, gz_ref, o_ref, st_scr):
    c = HG_CHUNK

    @pl.when(pl.program_id(2) == 0)
    def _():
        st_scr[...] = jnp.zeros_like(st_scr)

    lb = lb_ref[...]
    f = lb + (1.0 - lb) * _sigmoid(fz_ref[...])
    lf = jnp.log(f)
    kk = 1.0 - f
    qz = qz_ref[...]
    qq = qz * _sigmoid(qz)
    vv = iv_ref[...].astype(BF16)

    hi = lf.astype(BF16)
    lo = (lf - hi.astype(F32)).astype(BF16)
    e2 = _dot(stack_ref[...], jnp.concatenate([hi, lo], axis=1))
    ex = e2[:, :HEAD_DIM] + e2[:, HEAD_DIM:]

    level = level_ref[...]
    odd = (lax.broadcasted_iota(jnp.int32, (c, HEAD_DIM), 0) & 1) != 0
    a = jnp.zeros((c, c), F32)
    for j in range(HG_LEVELS):
        if j == 0:
            y = jnp.where(odd, f, 1.0)
        else:
            y = jnp.exp(ex[(j - 1) * c:j * c])
        p = _dot_nt((qq * y).astype(BF16), (kk * y).astype(BF16))
        a = jnp.where(level == j, p, a)
    a = jnp.where(level == HG_LEVELS, _dot_nt(qq.astype(BF16), kk.astype(BF16)), a)

    eb = jnp.exp(ex[(HG_LEVELS - 1) * c:HG_LEVELS * c])
    er = jnp.exp(ex[HG_LEVELS * c:(HG_LEVELS + 1) * c])
    st = st_scr[...]
    o = _dot_nt((qq * eb).astype(BF16), st.astype(BF16)) + _dot(a.astype(BF16), vv)
    st_scr[...] = st * eb[c - 1:c, :] + _dot_tn(vv, (kk * er).astype(BF16))

    o = o * lax.rsqrt(jnp.mean(o * o, axis=-1, keepdims=True) + RMS_EPS) * gain_ref[...]
    gz = gz_ref[...]
    o_ref[...] = (o * (gz * _sigmoid(gz))).astype(o_ref.dtype)


def _hgrn2(proj, lb, gain, batch, seq):
    stack, level = _hg_tables()
    c = HG_CHUNK
    blk = (None, c, HEAD_DIM)

    def part(g):
        return pl.BlockSpec(blk, lambda b, h, n: (b, n, g * HG_HEADS + h))

    vec = pl.BlockSpec((1, HEAD_DIM), lambda b, h, n: (0, h))
    return pl.pallas_call(
        _hgrn_kernel,
        grid=(batch, HG_HEADS, seq // c),
        in_specs=[pl.BlockSpec(stack.shape, lambda b, h, n: (0, 0)),
                  pl.BlockSpec(level.shape, lambda b, h, n: (0, 0)),
                  vec, vec, part(0), part(1), part(2), part(3)],
        out_specs=pl.BlockSpec(blk, lambda b, h, n: (b, n, h)),
        out_shape=jax.ShapeDtypeStruct((batch, seq, HG_WIDTH), BF16),
        scratch_shapes=[pltpu.VMEM((HEAD_DIM, HEAD_DIM), F32)],
        compiler_params=_params(("parallel", "parallel", "arbitrary")),
    )(jnp.asarray(stack, BF16), jnp.asarray(level), lb, gain, proj, proj, proj, proj)


def _mix_out_kernel(o1_ref, o2_ref, o3_ref, l1_ref, l2_ref, l3_ref, hg_ref, x_ref, g_ref, w_ref,
                    out_ref, a_scr):
    l1, l2, l3 = l1_ref[...], l2_ref[...], l3_ref[...]
    m = jnp.maximum(jnp.maximum(l1, l2), l3)
    e1, e2, e3 = jnp.exp(l1 - m), jnp.exp(l2 - m), jnp.exp(l3 - m)
    inv = 1.0 / (e1 + e2 + e3)
    w1, w2, w3 = e1 * inv, e2 * inv, e3 * inv
    for h in range(ATTN_HEADS):
        sl = slice(h * HEAD_DIM, (h + 1) * HEAD_DIM)
        a = (w1[:, h:h + 1] * o1_ref[:, sl].astype(F32) + w2[:, h:h + 1] * o2_ref[:, sl].astype(F32)
             + w3[:, h:h + 1] * o3_ref[:, sl].astype(F32))
        a_scr[:, sl] = a.astype(BF16)
    a_scr[:, ATTN_WIDTH:] = hg_ref[...]
    y = _dot(a_scr[...], w_ref[...])
    out_ref[...] = x_ref[...] + _rms(y, g_ref[...])


def _mix_out(outs, lses, hg, x, g, w, tm):
    m, d = x.shape
    row = lambda width: pl.BlockSpec((tm, width), lambda i: (i, 0))
    return pl.pallas_call(
        _mix_out_kernel,
        grid=(m // tm,),
        in_specs=[row(ATTN_WIDTH)] * 3 + [row(HEAD_DIM)] * 3 + [row(HG_WIDTH), row(d),
                  _const_spec((1, d)), _const_spec(w.shape)],
        out_specs=row(d),
        out_shape=jax.ShapeDtypeStruct((m, d), F32),
        scratch_shapes=[pltpu.VMEM((tm, ATTN_WIDTH + HG_WIDTH), BF16)],
        compiler_params=_params(("parallel",)),
    )(*outs, *lses, hg, x, g, w)


def _cross_kernel(x_ref, gq_ref, go_ref, wq_ref, kv_ref, wo_ref, out_ref, co_scr):
    x = x_ref[...]
    d = x.shape[-1]
    hd = d // CROSS_HEADS
    hc = _rms(x, gq_ref[...]).astype(BF16)
    cq = _dot(hc, wq_ref[...]).astype(BF16)
    scale = hd ** -0.5
    for h in range(CROSS_HEADS):
        sl = slice(h * hd, (h + 1) * hd)
        s = _dot_nt(cq[:, sl], kv_ref[:, sl]) * scale
        p = jnp.exp(s - jnp.max(s, axis=-1, keepdims=True))
        p = p / jnp.sum(p, axis=-1, keepdims=True)
        co_scr[:, sl] = _dot(p.astype(BF16), kv_ref[:, d + h * hd:d + (h + 1) * hd]).astype(BF16)
    y = _dot(co_scr[...], wo_ref[...])
    out_ref[...] = x + _rms(y, go_ref[...])


def _cross(x, gq, go, wq, kv, wo, seq, tm):
    m, d = x.shape
    n_mem = kv.shape[1]
    per_batch = seq // tm
    return pl.pallas_call(
        _cross_kernel,
        grid=(m // tm,),
        in_specs=[pl.BlockSpec((tm, d), lambda i: (i, 0)),
                  _const_spec((1, d)), _const_spec((1, d)), _const_spec(wq.shape),
                  pl.BlockSpec((None, n_mem, 2 * d), lambda i: (i // per_batch, 0, 0)),
                  _const_spec(wo.shape)],
        out_specs=pl.BlockSpec((tm, d), lambda i: (i, 0)),
        out_shape=jax.ShapeDtypeStruct((m, d), F32),
        scratch_shapes=[pltpu.VMEM((tm, d), BF16)],
        compiler_params=_params(("parallel",)),
    )(x, gq, go, wq, kv, wo)


def _ffn_kernel(x_ref, gi_ref, go_ref, wg_ref, wu_ref, wd_ref, out_ref, h_scr, acc_scr):
    j = pl.program_id(1)

    @pl.when(j == 0)
    def _():
        h_scr[...] = _rms(x_ref[...], gi_ref[...]).astype(BF16)
        acc_scr[...] = jnp.zeros_like(acc_scr)

    h = h_scr[...]
    gate = _dot(h, wg_ref[...])
    up = _dot(h, wu_ref[...])
    act = (gate * _sigmoid(gate) * up).astype(BF16)
    acc_scr[...] += _dot(act, wd_ref[...])

    @pl.when(j == pl.num_programs(1) - 1)
    def _():
        out_ref[...] = x_ref[...] + _rms(acc_scr[...], go_ref[...])


def _ffn(x, gi, go, w_gate_up, w_down, tm, tf):
    m, d = x.shape
    d_ff = w_down.shape[0]
    nf = d_ff // tf
    return pl.pallas_call(
        _ffn_kernel,
        grid=(m // tm, nf),
        in_specs=[pl.BlockSpec((tm, d), lambda i, j: (i, 0)),
                  _const_spec((1, d)), _const_spec((1, d)),
                  pl.BlockSpec((d, tf), lambda i, j: (0, j)),
                  pl.BlockSpec((d, tf), lambda i, j: (0, nf + j)),
                  pl.BlockSpec((tf, d), lambda i, j: (j, 0))],
        out_specs=pl.BlockSpec((tm, d), lambda i, j: (i, 0)),
        out_shape=jax.ShapeDtypeStruct((m, d), F32),
        scratch_shapes=[pltpu.VMEM((tm, d), BF16), pltpu.VMEM((tm, d), F32)],
        compiler_params=_params(("parallel", "arbitrary")),
    )(x, gi, go, w_gate_up, w_gate_up, w_down)


def kernel(x, mem, rel_bias, lb_logits, norm_gains, w_in, hg_norm, w_out, w_cq, w_ckv, w_co,
           w_gate_up, w_down):
    batch, seq, d = x.shape
    n_mem = mem.shape[1]
    depth = w_in.shape[0]
    lb_all = _lower_bounds(lb_logits)
    rel_bias = rel_bias.astype(F32)
    xf = x.reshape(batch * seq, d).astype(F32)
    memf = mem.reshape(batch * n_mem, d).astype(F32)
    qkv_cols = 3 * ATTN_WIDTH

    for l in range(depth):
        g = norm_gains[l].astype(F32)
        gain = lambda i: g[i:i + 1]
        w_in_l = w_in[l].astype(BF16)
        qkv = _norm_matmul(xf, gain(0), w_in_l[:, :qkv_cols], BF16, tm=1024, tn=1024)
        hproj = _norm_matmul(xf, gain(0), w_in_l[:, qkv_cols:], F32, tm=1024, tn=1024)
        outs, lses = [], []
        for window, dilation in DILATED_BRANCHES:
            o, lse = _dilated_branch(qkv, rel_bias, batch, seq, window, dilation)
            outs.append(o)
            lses.append(lse)
        hg = _hgrn2(hproj.reshape(batch, seq, 4 * HG_WIDTH), lb_all[l:l + 1],
                    hg_norm[l:l + 1].astype(F32), batch, seq)
        xf = _mix_out(outs, lses, hg.reshape(batch * seq, HG_WIDTH), xf, gain(1),
                      w_out[l].astype(BF16), tm=512)
        kv = _norm_matmul(memf, gain(3), w_ckv[l].astype(BF16), BF16, tm=batch * n_mem, tn=1024)
        xf = _cross(xf, gain(2), gain(4), w_cq[l].astype(BF16), kv.reshape(batch, n_mem, 2 * d),
                    w_co[l].astype(BF16), seq, tm=512)
        xf = _ffn(xf, gain(5), gain(6), w_gate_up[l].astype(BF16), w_down[l].astype(BF16),
                  tm=512, tf=512)
    return xf.reshape(batch, seq, d).astype(x.dtype)
```

```python
import math

import numpy as np
import jax
import jax.numpy as jnp
from jax import lax
from jax.experimental import pallas as pl
from jax.experimental.pallas import tpu as pltpu

F32 = jnp.float32
BF16 = jnp.bfloat16

HEAD_DIM = 128
ATTN_HEADS = 8
HG_HEADS = 8
ATTN_WIDTH = ATTN_HEADS * HEAD_DIM
HG_WIDTH = HG_HEADS * HEAD_DIM
Q_BLOCK = 128
DILATED_BRANCHES = ((128, 1), (512, 4), (2048, 16))
ATTN_SPAN = Q_BLOCK * max(d for _, d in DILATED_BRANCHES)
REL_BUCKETS = 32
REL_MAX_DIST = 2048
CROSS_HEADS = 4
RMS_EPS = 1e-6
NEG_INF = -1e30
HG_CHUNK = 128
HG_LEVELS = 7
VMEM_LIMIT = 52 * 1024 * 1024


def _rms(x, g):
    return x * lax.rsqrt(jnp.mean(x * x, axis=-1, keepdims=True) + RMS_EPS) * g


def _sigmoid(x):
    return 1.0 / (1.0 + jnp.exp(-x))


def _dot(a, b):
    return jnp.dot(a, b, preferred_element_type=F32)


def _dot_nt(a, b):
    return lax.dot_general(a, b, (((1,), (1,)), ((), ())), preferred_element_type=F32)


def _dot_tn(a, b):
    return lax.dot_general(a, b, (((0,), (0,)), ((), ())), preferred_element_type=F32)


def _params(sem):
    return pltpu.CompilerParams(dimension_semantics=sem, vmem_limit_bytes=VMEM_LIMIT)


def _const_spec(shape):
    nd = len(shape)
    return pl.BlockSpec(shape, lambda *_: (0,) * nd, pipeline_mode=pl.Buffered(1))


def _lb_kernel(z_ref, o_ref):
    z = z_ref[...]
    e = jnp.exp(z - jnp.max(z, axis=0, keepdims=True))
    p = e / jnp.sum(e, axis=0, keepdims=True)
    acc = jnp.zeros_like(p[0:1])
    for l in range(z.shape[0]):
        o_ref[l:l + 1, :] = acc
        acc = acc + p[l:l + 1]


def _lower_bounds(lb_logits):
    return pl.pallas_call(
        _lb_kernel, out_shape=jax.ShapeDtypeStruct(lb_logits.shape, F32),
        name="lower_bounds")(lb_logits.astype(F32))


def _norm_matmul_kernel(x_ref, g_ref, w_ref, o_ref, h_scr):
    @pl.when(pl.program_id(1) == 0)
    def _():
        h_scr[...] = _rms(x_ref[...], g_ref[...]).astype(BF16)

    o_ref[...] = _dot(h_scr[...], w_ref[...]).astype(o_ref.dtype)


def _norm_matmul(x, g, w, out_dtype, tm, tn, name):
    m, d = x.shape
    n = w.shape[1]
    return pl.pallas_call(
        _norm_matmul_kernel,
        grid=(m // tm, n // tn),
        in_specs=[pl.BlockSpec((tm, d), lambda i, j: (i, 0)),
                  pl.BlockSpec((1, d), lambda i, j: (0, 0)),
                  pl.BlockSpec((d, tn), lambda i, j: (0, j))],
        out_specs=pl.BlockSpec((tm, tn), lambda i, j: (i, j)),
        out_shape=jax.ShapeDtypeStruct((m, n), out_dtype),
        scratch_shapes=[pltpu.VMEM((tm, d), BF16)],
        compiler_params=_params(("parallel", "arbitrary")),
        name=name,
    )(x, g, w)


def _bucket_tables():
    qi = np.arange(Q_BLOCK)[:, None]
    kj = np.arange(2 * Q_BLOCK)[None, :]
    m = qi - kj + Q_BLOCK
    max_exact = REL_BUCKETS // 2
    tables = []
    for window, dilation in DILATED_BRANCHES:
        dist = np.maximum(m, 0) * dilation
        d_f = np.maximum(dist, 1).astype(np.float32)
        large = max_exact + (np.log(d_f / np.float32(max_exact))
                             / np.float32(math.log(REL_MAX_DIST / max_exact))
                             * (REL_BUCKETS - max_exact)).astype(np.int32)
        large = np.minimum(large, REL_BUCKETS - 1)
        bucket = np.where(dist < max_exact, dist, large)
        tables.append(np.where((m >= 0) & (m <= window // dilation), bucket, -1))
    return np.stack(tables).astype(np.int32)


def _dilated_kernel(rb_ref, bucket_ref, q_ref, k_ref, v_ref, o_ref,
                    bias_scr, k_scr, v_scr, acc_scr, m_scr, l_scr):
    h = pl.program_id(1)
    j = pl.program_id(2)
    span = ATTN_SPAN

    @pl.when(j == 0)
    def _():
        k_scr[0:span, :] = jnp.zeros((span, HEAD_DIM), F32)
        v_scr[0:span, :] = jnp.zeros((span, HEAD_DIM), F32)
        for g in range(len(DILATED_BRANCHES)):
            bucket = bucket_ref[g]
            acc = jnp.full(bucket.shape, NEG_INF, F32)
            for b in range(REL_BUCKETS):
                acc = jnp.where(bucket == b, rb_ref[b, h], acc)
            bias_scr[g] = acc

    k_scr[span:2 * span, :] = k_ref[...]
    v_scr[span:2 * span, :] = v_ref[...]

    col = lax.broadcasted_iota(jnp.int32, (Q_BLOCK, 2 * Q_BLOCK), 1)
    before_start = (j == 0) & (col < Q_BLOCK)
    ones = jnp.ones((2 * Q_BLOCK, HEAD_DIM), BF16)
    scale = HEAD_DIM ** -0.5
    for g, (_, d) in enumerate(DILATED_BRANCHES):
        for i in range(span // Q_BLOCK):
            r, n = i % d, i // d
            start = n * Q_BLOCK * d + r
            if d == 1:
                rows_q = pl.ds(start, Q_BLOCK)
                rows_kv = pl.ds(span + start - Q_BLOCK, 2 * Q_BLOCK)
            else:
                rows_q = pl.ds(start, Q_BLOCK, stride=d)
                rows_kv = pl.ds(span + start - Q_BLOCK * d, 2 * Q_BLOCK, stride=d)
            q = (q_ref[rows_q, :] * scale).astype(BF16)
            k = k_scr[rows_kv, :].astype(BF16)
            v = v_scr[rows_kv, :].astype(BF16)
            s = _dot_nt(q, k) + bias_scr[g]
            if n == 0:
                s = jnp.where(before_start, NEG_INF, s)
            m = jnp.max(s, axis=-1, keepdims=True)
            p = jnp.exp(s - m).astype(BF16)
            pv = _dot(p, jnp.concatenate([v, ones], axis=1))
            acc_scr[g, rows_q, :] = pv[:, :HEAD_DIM]
            l_scr[g, rows_q, :] = pv[:, HEAD_DIM:]
            m_scr[g, rows_q, :] = jnp.broadcast_to(m, (Q_BLOCK, HEAD_DIM))

    k_scr[0:span, :] = k_scr[span:2 * span, :]
    v_scr[0:span, :] = v_scr[span:2 * span, :]

    def merge(t, carry):
        rows = pl.ds(pl.multiple_of(t * Q_BLOCK, Q_BLOCK), Q_BLOCK)
        ms = [m_scr[g, rows, :] for g in range(len(DILATED_BRANCHES))]
        top = jnp.maximum(jnp.maximum(ms[0], ms[1]), ms[2])
        num = jnp.zeros((Q_BLOCK, HEAD_DIM), F32)
        den = jnp.zeros((Q_BLOCK, HEAD_DIM), F32)
        for g in range(len(DILATED_BRANCHES)):
            e = jnp.exp(ms[g] - top)
            num = num + e * acc_scr[g, rows, :]
            den = den + e * l_scr[g, rows, :]
        o_ref[rows, :] = (num / den).astype(o_ref.dtype)
        return carry

    lax.fori_loop(0, span // Q_BLOCK, merge, 0)


def _dilated_attention(qkv, rel_bias, batch, seq):
    span = ATTN_SPAN
    nspan = seq // span
    nbr = len(DILATED_BRANCHES)
    buckets = jnp.asarray(_bucket_tables())

    def part(p):
        return pl.BlockSpec((span, HEAD_DIM), lambda b, h, j: (b * nspan + j, p * ATTN_HEADS + h))

    stat = pltpu.VMEM((nbr, span, HEAD_DIM), F32)
    return pl.pallas_call(
        _dilated_kernel,
        grid=(batch, ATTN_HEADS, nspan),
        in_specs=[pl.BlockSpec(memory_space=pltpu.SMEM),
                  pl.BlockSpec(buckets.shape, lambda b, h, j: (0, 0, 0)),
                  part(0), part(1), part(2)],
        out_specs=pl.BlockSpec((span, HEAD_DIM), lambda b, h, j: (b * nspan + j, h)),
        out_shape=jax.ShapeDtypeStruct((batch * seq, ATTN_WIDTH), BF16),
        scratch_shapes=[pltpu.VMEM((nbr, Q_BLOCK, 2 * Q_BLOCK), F32),
                        pltpu.VMEM((2 * span, HEAD_DIM), F32),
                        pltpu.VMEM((2 * span, HEAD_DIM), F32),
                        stat, stat, stat],
        compiler_params=_params(("arbitrary", "arbitrary", "arbitrary")),
        name="dilated_attention",
    )(rel_bias, buckets, qkv, qkv, qkv)


def _hg_tables():
    c = HG_CHUNK
    t = np.arange(c)[:, None]
    u = np.arange(c)[None, :]
    mats = []
    for j in range(1, HG_LEVELS):
        s = 1 << j
        upper = (t & s) != 0
        q_side = (u >= (t & ~(s - 1))) & (u <= t)
        k_side = (u > t) & (u <= (t | (s - 1)))
        mats.append(np.where(upper, q_side, k_side))
    mats.append(u <= t)
    mats.append(u > t)
    stack = np.concatenate(mats, axis=0).astype(np.float32)
    level = np.where(u < t, np.floor(np.log2(np.maximum(t ^ u, 1))).astype(np.int32), -1)
    level = np.where(u == t, HG_LEVELS, level).astype(np.int32)
    return stack, level


def _hgrn_head(stack, level, lb, gain, fz, iv, qz, gz, st):
    c = HG_CHUNK
    f = lb + (1.0 - lb) * _sigmoid(fz)
    lf = jnp.log(f)
    kk = 1.0 - f
    qq = qz * _sigmoid(qz)
    vv = iv.astype(BF16)

    hi = lf.astype(BF16)
    lo = (lf - hi.astype(F32)).astype(BF16)
    e2 = _dot(stack, jnp.concatenate([hi, lo], axis=1))
    ex = e2[:, :HEAD_DIM] + e2[:, HEAD_DIM:]

    odd = (lax.broadcasted_iota(jnp.int32, (c, HEAD_DIM), 0) & 1) != 0
    a = jnp.zeros((c, c), F32)
    for j in range(HG_LEVELS):
        if j == 0:
            y = jnp.where(odd, f, 1.0)
        else:
            y = jnp.exp(ex[(j - 1) * c:j * c])
        p = _dot_nt((qq * y).astype(BF16), (kk * y).astype(BF16))
        a = jnp.where(level == j, p, a)
    a = jnp.where(level == HG_LEVELS, _dot_nt(qq.astype(BF16), kk.astype(BF16)), a)

    eb = jnp.exp(ex[(HG_LEVELS - 1) * c:HG_LEVELS * c])
    er = jnp.exp(ex[HG_LEVELS * c:(HG_LEVELS + 1) * c])
    o = _dot_nt((qq * eb).astype(BF16), st.astype(BF16)) + _dot(a.astype(BF16), vv)
    new_st = st * eb[c - 1:c, :] + _dot_tn(vv, (kk * er).astype(BF16))

    o = o * lax.rsqrt(jnp.mean(o * o, axis=-1, keepdims=True) + RMS_EPS) * gain
    return o * (gz * _sigmoid(gz)), new_st


def _hgrn_kernel(stack_ref, level_ref, lb_ref, gain_ref, fz_ref, iv_ref, qz_ref, gz_ref, o_ref, st_scr):
    @pl.when(pl.program_id(1) == 0)
    def _():
        st_scr[...] = jnp.zeros_like(st_scr)

    for h in range(HG_HEADS):
        sl = slice(h * HEAD_DIM, (h + 1) * HEAD_DIM)
        o, st = _hgrn_head(stack_ref[...], level_ref[...], lb_ref[:, sl], gain_ref[:, sl],
                           fz_ref[:, sl], iv_ref[:, sl], qz_ref[:, sl], gz_ref[:, sl], st_scr[h])
        st_scr[h] = st
        o_ref[:, sl] = o.astype(o_ref.dtype)


def _hgrn2(proj, lb, gain, batch, seq):
    stack, level = _hg_tables()
    c = HG_CHUNK
    nchunk = seq // c
    first = 3 * ATTN_WIDTH // HG_WIDTH

    def part(g):
        return pl.BlockSpec((c, HG_WIDTH), lambda b, n: (b * nchunk + n, first + g))

    return pl.pallas_call(
        _hgrn_kernel,
        grid=(batch, nchunk),
        in_specs=[_const_spec(stack.shape), _const_spec(level.shape),
                  _const_spec((1, HG_WIDTH)), _const_spec((1, HG_WIDTH)),
                  part(0), part(1), part(2), part(3)],
        out_specs=pl.BlockSpec((c, HG_WIDTH), lambda b, n: (b * nchunk + n, 0)),
        out_shape=jax.ShapeDtypeStruct((batch * seq, HG_WIDTH), BF16),
        scratch_shapes=[pltpu.VMEM((HG_HEADS, HEAD_DIM, HEAD_DIM), F32)],
        compiler_params=_params(("parallel", "arbitrary")),
        name="hgrn2",
    )(jnp.asarray(stack, BF16), jnp.asarray(level), lb, gain, proj, proj, proj, proj)


def _mix_out_kernel(attn_ref, hg_ref, x_ref, g_ref, w_ref, out_ref):
    y = _dot(attn_ref[...], w_ref[:ATTN_WIDTH, :]) + _dot(hg_ref[...], w_ref[ATTN_WIDTH:, :])
    out_ref[...] = x_ref[...] + _rms(y, g_ref[...])


def _mix_out(attn, hg, x, g, w, tm):
    m, d = x.shape
    row = lambda width: pl.BlockSpec((tm, width), lambda i: (i, 0))
    return pl.pallas_call(
        _mix_out_kernel,
        grid=(m // tm,),
        in_specs=[row(ATTN_WIDTH), row(HG_WIDTH), row(d), _const_spec((1, d)), _const_spec(w.shape)],
        out_specs=row(d),
        out_shape=jax.ShapeDtypeStruct((m, d), F32),
        compiler_params=_params(("parallel",)),
        name="mix_out",
    )(attn, hg, x, g, w)


def _cross_kernel(x_ref, gq_ref, go_ref, wq_ref, kv_ref, wo_ref, out_ref, co_scr):
    x = x_ref[...]
    d = x.shape[-1]
    hd = d // CROSS_HEADS
    hc = _rms(x, gq_ref[...]).astype(BF16)
    cq = _dot(hc, wq_ref[...]).astype(BF16)
    scale = hd ** -0.5
    for h in range(CROSS_HEADS):
        sl = slice(h * hd, (h + 1) * hd)
        s = _dot_nt(cq[:, sl], kv_ref[:, sl]) * scale
        p = jnp.exp(s - jnp.max(s, axis=-1, keepdims=True))
        p = p / jnp.sum(p, axis=-1, keepdims=True)
        co_scr[:, sl] = _dot(p.astype(BF16), kv_ref[:, d + h * hd:d + (h + 1) * hd]).astype(BF16)
    y = _dot(co_scr[...], wo_ref[...])
    out_ref[...] = x + _rms(y, go_ref[...])


def _cross(x, gq, go, wq, kv, wo, seq, tm):
    m, d = x.shape
    n_mem = kv.shape[1]
    per_batch = seq // tm
    return pl.pallas_call(
        _cross_kernel,
        grid=(m // tm,),
        in_specs=[pl.BlockSpec((tm, d), lambda i: (i, 0)),
                  _const_spec((1, d)), _const_spec((1, d)), _const_spec(wq.shape),
                  pl.BlockSpec((None, n_mem, 2 * d), lambda i: (i // per_batch, 0, 0)),
                  _const_spec(wo.shape)],
        out_specs=pl.BlockSpec((tm, d), lambda i: (i, 0)),
        out_shape=jax.ShapeDtypeStruct((m, d), F32),
        scratch_shapes=[pltpu.VMEM((tm, d), BF16)],
        compiler_params=_params(("parallel",)),
        name="cross_attention",
    )(x, gq, go, wq, kv, wo)


def _ffn_kernel(x_ref, gi_ref, go_ref, wg_ref, wu_ref, wd_ref, out_ref, h_scr, acc_scr):
    j = pl.program_id(1)

    @pl.when(j == 0)
    def _():
        h_scr[...] = _rms(x_ref[...], gi_ref[...]).astype(BF16)
        acc_scr[...] = jnp.zeros_like(acc_scr)

    h = h_scr[...]
    gate = _dot(h, wg_ref[...])
    up = _dot(h, wu_ref[...])
    act = (gate * _sigmoid(gate) * up).astype(BF16)
    acc_scr[...] += _dot(act, wd_ref[...])

    @pl.when(j == pl.num_programs(1) - 1)
    def _():
        out_ref[...] = x_ref[...] + _rms(acc_scr[...], go_ref[...])


def _ffn(x, gi, go, w_gate_up, w_down, tm, tf):
    m, d = x.shape
    d_ff = w_down.shape[0]
    nf = d_ff // tf
    return pl.pallas_call(
        _ffn_kernel,
        grid=(m // tm, nf),
        in_specs=[pl.BlockSpec((tm, d), lambda i, j: (i, 0)),
                  _const_spec((1, d)), _const_spec((1, d)),
                  pl.BlockSpec((d, tf), lambda i, j: (0, j)),
                  pl.BlockSpec((d, tf), lambda i, j: (0, nf + j)),
                  pl.BlockSpec((tf, d), lambda i, j: (j, 0))],
        out_specs=pl.BlockSpec((tm, d), lambda i, j: (i, 0)),
        out_shape=jax.ShapeDtypeStruct((m, d), F32),
        scratch_shapes=[pltpu.VMEM((tm, d), BF16), pltpu.VMEM((tm, d), F32)],
        compiler_params=_params(("parallel", "arbitrary")),
        name="swiglu_ffn",
    )(x, gi, go, w_gate_up, w_gate_up, w_down)


def kernel(x, mem, rel_bias, lb_logits, norm_gains, w_in, hg_norm, w_out, w_cq, w_ckv, w_co,
           w_gate_up, w_down):
    batch, seq, d = x.shape
    n_mem = mem.shape[1]
    depth = w_in.shape[0]
    lb_all = _lower_bounds(lb_logits)
    rel_bias = rel_bias.astype(F32)
    xf = x.reshape(batch * seq, d).astype(F32)
    memf = mem.reshape(batch * n_mem, d).astype(F32)

    for l in range(depth):
        g = norm_gains[l].astype(F32)
        gain = lambda i: g[i:i + 1]
        proj = _norm_matmul(xf, gain(0), w_in[l].astype(BF16), F32, tm=1024, tn=1024, name="in_proj")
        attn = _dilated_attention(proj, rel_bias, batch, seq)
        hg = _hgrn2(proj, lb_all[l:l + 1], hg_norm[l:l + 1].astype(F32), batch, seq)
        xf = _mix_out(attn, hg, xf, gain(1), w_out[l].astype(BF16), tm=512)
        kv = _norm_matmul(memf, gain(3), w_ckv[l].astype(BF16), BF16, tm=batch * n_mem, tn=1024,
                          name="mem_kv_proj")
        xf = _cross(xf, gain(2), gain(4), w_cq[l].astype(BF16), kv.reshape(batch, n_mem, 2 * d),
                    w_co[l].astype(BF16), seq, tm=512)
        xf = _ffn(xf, gain(5), gain(6), w_gate_up[l].astype(BF16), w_down[l].astype(BF16),
                  tm=512, tf=512)
    return xf.reshape(batch, seq, d).astype(x.dtype)
```

```python
import math

import numpy as np
import jax
import jax.numpy as jnp
from jax import lax
from jax.experimental import pallas as pl
from jax.experimental.pallas import tpu as pltpu

F32 = jnp.float32
BF16 = jnp.bfloat16

HEAD_DIM = 128
ATTN_HEADS = 8
HG_HEADS = 8
ATTN_WIDTH = ATTN_HEADS * HEAD_DIM
HG_WIDTH = HG_HEADS * HEAD_DIM
Q_BLOCK = 128
DILATED_BRANCHES = ((128, 1), (512, 4), (2048, 16))
ATTN_SPAN = Q_BLOCK * max(d for _, d in DILATED_BRANCHES)
REL_BUCKETS = 32
REL_MAX_DIST = 2048
CROSS_HEADS = 4
RMS_EPS = 1e-6
NEG_INF = -1e30
HG_CHUNK = 128
HG_LEVELS = 7
LOG2_E = 1.4426950408889634
VMEM_LIMIT = 52 * 1024 * 1024


def _rms(x, g):
    return x * lax.rsqrt(jnp.mean(x * x, axis=-1, keepdims=True) + RMS_EPS) * g


def _sigmoid(x):
    return 1.0 / (1.0 + jnp.exp(-x))


def _dot(a, b):
    return jnp.dot(a, b, preferred_element_type=F32)


def _dot_nt(a, b):
    return lax.dot_general(a, b, (((1,), (1,)), ((), ())), preferred_element_type=F32)


def _dot_tn(a, b):
    return lax.dot_general(a, b, (((0,), (0,)), ((), ())), preferred_element_type=F32)


def _params(sem):
    return pltpu.CompilerParams(dimension_semantics=sem, vmem_limit_bytes=VMEM_LIMIT)


def _const_spec(shape):
    nd = len(shape)
    return pl.BlockSpec(shape, lambda *_: (0,) * nd, pipeline_mode=pl.Buffered(1))


def _layer_spec(shape, layer):
    nd = len(shape)
    return pl.BlockSpec((None,) + tuple(shape), lambda *_: (layer,) + (0,) * nd,
                        pipeline_mode=pl.Buffered(1))


def _lb_kernel(z_ref, o_ref):
    z = z_ref[...]
    e = jnp.exp(z - jnp.max(z, axis=0, keepdims=True))
    p = e / jnp.sum(e, axis=0, keepdims=True)
    acc = jnp.zeros_like(p[0:1])
    for l in range(z.shape[0]):
        o_ref[l:l + 1, :] = acc
        acc = acc + p[l:l + 1]


def _lower_bounds(lb_logits):
    return pl.pallas_call(
        _lb_kernel, out_shape=jax.ShapeDtypeStruct(lb_logits.shape, F32),
        name="lower_bounds")(lb_logits.astype(F32))


def _norm_matmul_kernel(x_ref, g_ref, w_ref, o_ref, h_scr):
    @pl.when(pl.program_id(1) == 0)
    def _():
        h_scr[...] = _rms(x_ref[...], g_ref[...]).astype(BF16)

    o_ref[...] = _dot(h_scr[...], w_ref[...]).astype(o_ref.dtype)


def _norm_matmul(x, gains, gi, w, layer, out_dtype, tm, tn, name):
    m, d = x.shape
    n = w.shape[2]
    return pl.pallas_call(
        _norm_matmul_kernel,
        grid=(m // tm, n // tn),
        in_specs=[pl.BlockSpec((tm, d), lambda i, j: (i, 0)),
                  _layer_spec((1, d), gi),
                  pl.BlockSpec((None, d, tn), lambda i, j: (layer, 0, j))],
        out_specs=pl.BlockSpec((tm, tn), lambda i, j: (i, j)),
        out_shape=jax.ShapeDtypeStruct((m, n), out_dtype),
        scratch_shapes=[pltpu.VMEM((tm, d), BF16)],
        compiler_params=_params(("parallel", "arbitrary")),
        name=name,
    )(x, gains, w)


def _bucket_tables():
    qi = np.arange(Q_BLOCK)[:, None]
    kj = np.arange(2 * Q_BLOCK)[None, :]
    m = qi - kj + Q_BLOCK
    max_exact = REL_BUCKETS // 2
    tables = []
    for window, dilation in DILATED_BRANCHES:
        dist = np.maximum(m, 0) * dilation
        d_f = np.maximum(dist, 1).astype(np.float32)
        large = max_exact + (np.log(d_f / np.float32(max_exact))
                             / np.float32(math.log(REL_MAX_DIST / max_exact))
                             * (REL_BUCKETS - max_exact)).astype(np.int32)
        large = np.minimum(large, REL_BUCKETS - 1)
        bucket = np.where(dist < max_exact, dist, large)
        tables.append(np.where((m >= 0) & (m <= window // dilation), bucket, -1))
    return np.stack(tables).astype(np.int32)


def _dilated_kernel(rb_ref, bucket_ref, q_ref, k_ref, v_ref, o_ref,
                    bias_scr, k_scr, v_scr, acc_scr, m_scr, l_scr):
    h = pl.program_id(1)
    j = pl.program_id(2)
    span = ATTN_SPAN

    @pl.when(j == 0)
    def _():
        k_scr[0:span, :] = jnp.zeros((span, HEAD_DIM), F32)
        v_scr[0:span, :] = jnp.zeros((span, HEAD_DIM), F32)
        for g in range(len(DILATED_BRANCHES)):
            bucket = bucket_ref[g]
            acc = jnp.full(bucket.shape, NEG_INF, F32)
            for b in range(REL_BUCKETS):
                acc = jnp.where(bucket == b, rb_ref[b, h], acc)
            bias_scr[g] = acc

    k_scr[span:2 * span, :] = k_ref[...]
    v_scr[span:2 * span, :] = v_ref[...]

    col = lax.broadcasted_iota(jnp.int32, (Q_BLOCK, 2 * Q_BLOCK), 1)
    before_start = (j == 0) & (col < Q_BLOCK)
    ones = jnp.ones((2 * Q_BLOCK, HEAD_DIM), BF16)
    scale = HEAD_DIM ** -0.5
    for g, (_, d) in enumerate(DILATED_BRANCHES):
        for i in range(span // Q_BLOCK):
            r, n = i % d, i // d
            start = n * Q_BLOCK * d + r
            if d == 1:
                rows_q = pl.ds(start, Q_BLOCK)
                rows_kv = pl.ds(span + start - Q_BLOCK, 2 * Q_BLOCK)
            else:
                rows_q = pl.ds(start, Q_BLOCK, stride=d)
                rows_kv = pl.ds(span + start - Q_BLOCK * d, 2 * Q_BLOCK, stride=d)
            q = (q_ref[rows_q, :] * scale).astype(BF16)
            k = k_scr[rows_kv, :].astype(BF16)
            v = v_scr[rows_kv, :].astype(BF16)
            s = _dot_nt(q, k) + bias_scr[g]
            if n == 0:
                s = jnp.where(before_start, NEG_INF, s)
            m = jnp.max(s, axis=-1, keepdims=True)
            p = jnp.exp(s - m).astype(BF16)
            pv = _dot(p, jnp.concatenate([v, ones], axis=1))
            acc_scr[g, rows_q, :] = pv[:, :HEAD_DIM]
            l_scr[g, rows_q, :] = pv[:, HEAD_DIM:]
            m_scr[g, rows_q, :] = jnp.broadcast_to(m, (Q_BLOCK, HEAD_DIM))

    k_scr[0:span, :] = k_scr[span:2 * span, :]
    v_scr[0:span, :] = v_scr[span:2 * span, :]

    def merge(t, carry):
        rows = pl.ds(pl.multiple_of(t * Q_BLOCK, Q_BLOCK), Q_BLOCK)
        ms = [m_scr[g, rows, :] for g in range(len(DILATED_BRANCHES))]
        top = jnp.maximum(jnp.maximum(ms[0], ms[1]), ms[2])
        num = jnp.zeros((Q_BLOCK, HEAD_DIM), F32)
        den = jnp.zeros((Q_BLOCK, HEAD_DIM), F32)
        for g in range(len(DILATED_BRANCHES)):
            e = jnp.exp(ms[g] - top)
            num = num + e * acc_scr[g, rows, :]
            den = den + e * l_scr[g, rows, :]
        o_ref[rows, :] = (num / den).astype(o_ref.dtype)
        return carry

    lax.fori_loop(0, span // Q_BLOCK, merge, 0)


def _dilated_attention(qkv, rel_bias, batch, seq):
    span = ATTN_SPAN
    nspan = seq // span
    nbr = len(DILATED_BRANCHES)
    buckets = jnp.asarray(_bucket_tables())

    def part(p):
        return pl.BlockSpec((span, HEAD_DIM), lambda b, h, j: (b * nspan + j, p * ATTN_HEADS + h))

    stat = pltpu.VMEM((nbr, span, HEAD_DIM), F32)
    return pl.pallas_call(
        _dilated_kernel,
        grid=(batch, ATTN_HEADS, nspan),
        in_specs=[pl.BlockSpec(memory_space=pltpu.SMEM),
                  pl.BlockSpec(buckets.shape, lambda b, h, j: (0, 0, 0)),
                  part(0), part(1), part(2)],
        out_specs=pl.BlockSpec((span, HEAD_DIM), lambda b, h, j: (b * nspan + j, h)),
        out_shape=jax.ShapeDtypeStruct((batch * seq, ATTN_WIDTH), BF16),
        scratch_shapes=[pltpu.VMEM((nbr, Q_BLOCK, 2 * Q_BLOCK), F32),
                        pltpu.VMEM((2 * span, HEAD_DIM), F32),
                        pltpu.VMEM((2 * span, HEAD_DIM), F32),
                        stat, stat, stat],
        compiler_params=_params(("arbitrary", "arbitrary", "arbitrary")),
        name="dilated_attention",
    )(rel_bias, buckets, qkv, qkv, qkv)


def _hg_tables():
    c = HG_CHUNK
    t = np.arange(c)[:, None]
    u = np.arange(c)[None, :]
    mats = []
    for j in range(1, HG_LEVELS):
        s = 1 << j
        upper = (t & s) != 0
        q_side = (u >= (t & ~(s - 1))) & (u <= t)
        k_side = (u > t) & (u <= (t | (s - 1)))
        mats.append(np.where(upper, q_side, k_side))
    mats.append(u <= t)
    mats.append(u > t)
    stack = np.concatenate(mats, axis=0).astype(np.float32)
    level = np.where(u < t, np.floor(np.log2(np.maximum(t ^ u, 1))).astype(np.int32), -1)
    level = np.where(u == t, HG_LEVELS, level).astype(np.int32)
    return stack, level


def _hgrn_pair(stack2, level, lb, gain, fz, iv, qz, gz, states):
    c = HG_CHUNK
    halves = (slice(0, HEAD_DIM), slice(HEAD_DIM, 2 * HEAD_DIM))
    f = lb + (1.0 - lb) * _sigmoid(fz)
    lf = jnp.log(f) * LOG2_E
    kk = 1.0 - f
    qq = qz * _sigmoid(qz)
    vv = iv.astype(BF16)

    hi = lf.astype(BF16)
    lo = (lf - hi.astype(F32)).astype(BF16)
    ex = _dot(stack2, jnp.concatenate([hi, lo], axis=0))

    odd = (lax.broadcasted_iota(jnp.int32, f.shape, 0) & 1) != 0
    a = [jnp.zeros((c, c), F32) for _ in halves]
    for j in range(HG_LEVELS + 1):
        if j == 0:
            qy, ky = (qq * jnp.where(odd, f, 1.0)).astype(BF16), kk.astype(BF16)
        elif j < HG_LEVELS:
            y = jnp.exp2(ex[(j - 1) * c:j * c])
            qy, ky = (qq * y).astype(BF16), (kk * y).astype(BF16)
        else:
            qy, ky = qq.astype(BF16), kk.astype(BF16)
        for i, sl in enumerate(halves):
            a[i] = jnp.where(level == j, _dot_nt(qy[:, sl], ky[:, sl]), a[i])

    eb = jnp.exp2(ex[(HG_LEVELS - 1) * c:HG_LEVELS * c])
    er = jnp.exp2(ex[HG_LEVELS * c:(HG_LEVELS + 1) * c])
    qe = (qq * eb).astype(BF16)
    ke = (kk * er).astype(BF16)
    outs, new_states = [], []
    for i, sl in enumerate(halves):
        st = states[i]
        o = _dot_nt(qe[:, sl], st.astype(BF16)) + _dot(a[i].astype(BF16), vv[:, sl])
        new_states.append(st * eb[c - 1:c, sl] + _dot_tn(vv[:, sl], ke[:, sl]))
        outs.append(o * lax.rsqrt(jnp.mean(o * o, axis=-1, keepdims=True) + RMS_EPS))
    o = jnp.concatenate(outs, axis=1) * gain
    return o * (gz * _sigmoid(gz)), new_states


def _hgrn_kernel(stack_ref, level_ref, lb_ref, gain_ref, fz_ref, iv_ref, qz_ref, gz_ref, o_ref, st_scr):
    @pl.when(pl.program_id(1) == 0)
    def _():
        st_scr[...] = jnp.zeros_like(st_scr)

    for hp in range(HG_HEADS // 2):
        sl = slice(2 * hp * HEAD_DIM, 2 * (hp + 1) * HEAD_DIM)
        o, states = _hgrn_pair(stack_ref[...], level_ref[...], lb_ref[:, sl], gain_ref[:, sl],
                               fz_ref[:, sl], iv_ref[:, sl], qz_ref[:, sl], gz_ref[:, sl],
                               (st_scr[2 * hp], st_scr[2 * hp + 1]))
        st_scr[2 * hp] = states[0]
        st_scr[2 * hp + 1] = states[1]
        o_ref[:, sl] = o.astype(o_ref.dtype)


def _hgrn2(proj, lb, gain, layer, batch, seq):
    stack, level = _hg_tables()
    stack2 = np.concatenate([stack, stack], axis=1)
    c = HG_CHUNK
    nchunk = seq // c
    first = 3 * ATTN_WIDTH // HG_WIDTH

    def part(g):
        return pl.BlockSpec((c, HG_WIDTH), lambda b, n: (b * nchunk + n, first + g))

    return pl.pallas_call(
        _hgrn_kernel,
        grid=(batch, nchunk),
        in_specs=[_const_spec(stack2.shape), _const_spec(level.shape),
                  _layer_spec((1, HG_WIDTH), layer), _layer_spec((1, HG_WIDTH), layer),
                  part(0), part(1), part(2), part(3)],
        out_specs=pl.BlockSpec((c, HG_WIDTH), lambda b, n: (b * nchunk + n, 0)),
        out_shape=jax.ShapeDtypeStruct((batch * seq, HG_WIDTH), BF16),
        scratch_shapes=[pltpu.VMEM((HG_HEADS, HEAD_DIM, HEAD_DIM), F32)],
        compiler_params=_params(("parallel", "arbitrary")),
        name="hgrn2",
    )(jnp.asarray(stack2, BF16), jnp.asarray(level), lb, gain, proj, proj, proj, proj)


def _mix_out_kernel(attn_ref, hg_ref, x_ref, g_ref, w_ref, out_ref):
    y = _dot(attn_ref[...], w_ref[:ATTN_WIDTH, :]) + _dot(hg_ref[...], w_ref[ATTN_WIDTH:, :])
    out_ref[...] = x_ref[...] + _rms(y, g_ref[...])


def _mix_out(attn, hg, x, gains, gi, w, layer, tm):
    m, d = x.shape
    row = lambda width: pl.BlockSpec((tm, width), lambda i: (i, 0))
    return pl.pallas_call(
        _mix_out_kernel,
        grid=(m // tm,),
        in_specs=[row(ATTN_WIDTH), row(HG_WIDTH), row(d), _layer_spec((1, d), gi),
                  _layer_spec(w.shape[1:], layer)],
        out_specs=row(d),
        out_shape=jax.ShapeDtypeStruct((m, d), F32),
        compiler_params=_params(("parallel",)),
        name="mix_out",
    )(attn, hg, x, gains, w)


def _cross_kernel(x_ref, gq_ref, go_ref, wq_ref, kv_ref, wo_ref, out_ref, co_scr):
    x = x_ref[...]
    d = x.shape[-1]
    hd = d // CROSS_HEADS
    hc = _rms(x, gq_ref[...]).astype(BF16)
    cq = _dot(hc, wq_ref[...]).astype(BF16)
    scale = hd ** -0.5
    for h in range(CROSS_HEADS):
        sl = slice(h * hd, (h + 1) * hd)
        s = _dot_nt(cq[:, sl], kv_ref[:, sl]) * scale
        p = jnp.exp(s - jnp.max(s, axis=-1, keepdims=True))
        p = p / jnp.sum(p, axis=-1, keepdims=True)
        co_scr[:, sl] = _dot(p.astype(BF16), kv_ref[:, d + h * hd:d + (h + 1) * hd]).astype(BF16)
    y = _dot(co_scr[...], wo_ref[...])
    out_ref[...] = x + _rms(y, go_ref[...])


def _cross(x, gains, gq, go, wq, kv, wo, layer, seq, tm):
    m, d = x.shape
    n_mem = kv.shape[1]
    per_batch = seq // tm
    return pl.pallas_call(
        _cross_kernel,
        grid=(m // tm,),
        in_specs=[pl.BlockSpec((tm, d), lambda i: (i, 0)),
                  _layer_spec((1, d), gq), _layer_spec((1, d), go), _layer_spec(wq.shape[1:], layer),
                  pl.BlockSpec((None, n_mem, 2 * d), lambda i: (i // per_batch, 0, 0)),
                  _layer_spec(wo.shape[1:], layer)],
        out_specs=pl.BlockSpec((tm, d), lambda i: (i, 0)),
        out_shape=jax.ShapeDtypeStruct((m, d), F32),
        scratch_shapes=[pltpu.VMEM((tm, d), BF16)],
        compiler_params=_params(("parallel",)),
        name="cross_attention",
    )(x, gains, gains, wq, kv, wo)


def _ffn_kernel(x_ref, gi_ref, go_ref, wg_ref, wu_ref, wd_ref, out_ref, h_scr, acc_scr):
    j = pl.program_id(1)

    @pl.when(j == 0)
    def _():
        h_scr[...] = _rms(x_ref[...], gi_ref[...]).astype(BF16)
        acc_scr[...] = jnp.zeros_like(acc_scr)

    h = h_scr[...]
    gate = _dot(h, wg_ref[...])
    up = _dot(h, wu_ref[...])
    act = (gate * _sigmoid(gate) * up).astype(BF16)
    acc_scr[...] += _dot(act, wd_ref[...])

    @pl.when(j == pl.num_programs(1) - 1)
    def _():
        out_ref[...] = x_ref[...] + _rms(acc_scr[...], go_ref[...])


def _ffn(x, gains, gi, go, w_gate_up, w_down, layer, tm, tf):
    m, d = x.shape
    d_ff = w_down.shape[1]
    nf = d_ff // tf
    return pl.pallas_call(
        _ffn_kernel,
        grid=(m // tm, nf),
        in_specs=[pl.BlockSpec((tm, d), lambda i, j: (i, 0)),
                  _layer_spec((1, d), gi), _layer_spec((1, d), go),
                  pl.BlockSpec((None, d, tf), lambda i, j: (layer, 0, j)),
                  pl.BlockSpec((None, d, tf), lambda i, j: (layer, 0, nf + j)),
                  pl.BlockSpec((None, tf, d), lambda i, j: (layer, j, 0))],
        out_specs=pl.BlockSpec((tm, d), lambda i, j: (i, 0)),
        out_shape=jax.ShapeDtypeStruct((m, d), F32),
        scratch_shapes=[pltpu.VMEM((tm, d), BF16), pltpu.VMEM((tm, d), F32)],
        compiler_params=_params(("parallel", "arbitrary")),
        name="swiglu_ffn",
    )(x, gains, gains, w_gate_up, w_gate_up, w_down)


def kernel(x, mem, rel_bias, lb_logits, norm_gains, w_in, hg_norm, w_out, w_cq, w_ckv, w_co,
           w_gate_up, w_down):
    batch, seq, d = x.shape
    n_mem = mem.shape[1]
    depth = w_in.shape[0]
    n_norms = norm_gains.shape[1]
    lb_all = _lower_bounds(lb_logits).reshape(depth, 1, HG_WIDTH)
    hg_gain = hg_norm.astype(F32).reshape(depth, 1, HG_WIDTH)
    gains = norm_gains.astype(F32).reshape(depth * n_norms, 1, d)
    rel_bias = rel_bias.astype(F32)
    xf = x.reshape(batch * seq, d).astype(F32)
    memf = mem.reshape(batch * n_mem, d).astype(F32)
    w_in, w_out, w_cq, w_ckv, w_co, w_gate_up, w_down = (
        w.astype(BF16) for w in (w_in, w_out, w_cq, w_ckv, w_co, w_gate_up, w_down))

    for l in range(depth):
        gi = lambda i: l * n_norms + i
        proj = _norm_matmul(xf, gains, gi(0), w_in, l, F32, tm=1024, tn=1024, name="in_proj")
        attn = _dilated_attention(proj, rel_bias, batch, seq)
        hg = _hgrn2(proj, lb_all, hg_gain, l, batch, seq)
        xf = _mix_out(attn, hg, xf, gains, gi(1), w_out, l, tm=512)
        kv = _norm_matmul(memf, gains, gi(3), w_ckv, l, BF16, tm=batch * n_mem, tn=1024,
                          name="mem_kv_proj")
        xf = _cross(xf, gains, gi(2), gi(4), w_cq, kv.reshape(batch, n_mem, 2 * d), w_co, l, seq, tm=512)
        xf = _ffn(xf, gains, gi(5), gi(6), w_gate_up, w_down, l, tm=512, tf=512)
    return xf.reshape(batch, seq, d).astype(x.dtype)
```

```python
import math

import numpy as np
import jax
import jax.numpy as jnp
from jax import lax
from jax.experimental import pallas as pl
from jax.experimental.pallas import tpu as pltpu

F32 = jnp.float32
BF16 = jnp.bfloat16

HEAD_DIM = 128
ATTN_HEADS = 8
HG_HEADS = 8
ATTN_WIDTH = ATTN_HEADS * HEAD_DIM
HG_WIDTH = HG_HEADS * HEAD_DIM
Q_BLOCK = 128
DILATED_BRANCHES = ((128, 1), (512, 4), (2048, 16))
ATTN_SPAN = Q_BLOCK * max(d for _, d in DILATED_BRANCHES)
GATHER_STRIDE = 4
REL_BUCKETS = 32
REL_MAX_DIST = 2048
CROSS_HEADS = 4
RMS_EPS = 1e-6
NEG_INF = -1e30
HG_CHUNK = 128
HG_LEVELS = 7
LOG2_E = 1.4426950408889634
VMEM_LIMIT = 52 * 1024 * 1024


def _rms(x, g):
    return x * lax.rsqrt(jnp.mean(x * x, axis=-1, keepdims=True) + RMS_EPS) * g


def _sigmoid(x):
    return 1.0 / (1.0 + jnp.exp(-x))


def _dot(a, b):
    return jnp.dot(a, b, preferred_element_type=F32)


def _dot_nt(a, b):
    return lax.dot_general(a, b, (((1,), (1,)), ((), ())), preferred_element_type=F32)


def _dot_tn(a, b):
    return lax.dot_general(a, b, (((0,), (0,)), ((), ())), preferred_element_type=F32)


def _params(sem):
    return pltpu.CompilerParams(dimension_semantics=sem, vmem_limit_bytes=VMEM_LIMIT)


def _const_spec(shape):
    nd = len(shape)
    return pl.BlockSpec(shape, lambda *_: (0,) * nd, pipeline_mode=pl.Buffered(1))


def _layer_spec(shape, layer):
    nd = len(shape)
    return pl.BlockSpec((None,) + tuple(shape), lambda *_: (layer,) + (0,) * nd,
                        pipeline_mode=pl.Buffered(1))


def _lb_kernel(z_ref, o_ref):
    z = z_ref[...]
    e = jnp.exp(z - jnp.max(z, axis=0, keepdims=True))
    p = e / jnp.sum(e, axis=0, keepdims=True)
    acc = jnp.zeros_like(p[0:1])
    for l in range(z.shape[0]):
        o_ref[l:l + 1, :] = acc
        acc = acc + p[l:l + 1]


def _lower_bounds(lb_logits):
    return pl.pallas_call(
        _lb_kernel, out_shape=jax.ShapeDtypeStruct(lb_logits.shape, F32),
        name="lower_bounds")(lb_logits.astype(F32))


def _norm_matmul_kernel(x_ref, g_ref, w_ref, o_ref, h_scr):
    j = pl.program_id(1)
    tm = x_ref.shape[0]

    @pl.when(j == 0)
    def _():
        rows = tm // 2
        for r in range(2):
            sl = slice(r * rows, (r + 1) * rows)
            h = _rms(x_ref[sl, :], g_ref[...]).astype(BF16)
            h_scr[sl, :] = h
            o_ref[sl, :] = _dot(h, w_ref[...]).astype(o_ref.dtype)

    @pl.when(j > 0)
    def _():
        o_ref[...] = _dot(h_scr[...], w_ref[...]).astype(o_ref.dtype)


def _norm_matmul(x, gains, gi, w, layer, out_dtype, tm, tn, name):
    m, d = x.shape
    n = w.shape[2]
    return pl.pallas_call(
        _norm_matmul_kernel,
        grid=(m // tm, n // tn),
        in_specs=[pl.BlockSpec((tm, d), lambda i, j: (i, 0)),
                  _layer_spec((1, d), gi),
                  pl.BlockSpec((None, d, tn), lambda i, j: (layer, 0, j))],
        out_specs=pl.BlockSpec((tm, tn), lambda i, j: (i, j)),
        out_shape=jax.ShapeDtypeStruct((m, n), out_dtype),
        scratch_shapes=[pltpu.VMEM((tm, d), BF16)],
        compiler_params=_params(("parallel", "arbitrary")),
        name=name,
    )(x, gains, w)


def _bucket_tables():
    qi = np.arange(Q_BLOCK)[:, None]
    kj = np.arange(2 * Q_BLOCK)[None, :]
    m = qi - kj + Q_BLOCK
    max_exact = REL_BUCKETS // 2
    tables = []
    for window, dilation in DILATED_BRANCHES:
        dist = np.maximum(m, 0) * dilation
        d_f = np.maximum(dist, 1).astype(np.float32)
        large = max_exact + (np.log(d_f / np.float32(max_exact))
                             / np.float32(math.log(REL_MAX_DIST / max_exact))
                             * (REL_BUCKETS - max_exact)).astype(np.int32)
        large = np.minimum(large, REL_BUCKETS - 1)
        bucket = np.where(dist < max_exact, dist, large)
        tables.append(np.where((m >= 0) & (m <= window // dilation), bucket, -1))
    return np.stack(tables).astype(np.int32)


def _dilated_kernel(rb_ref, bucket_ref, q_ref, k_ref, v_ref, o_ref,
                    bias_scr, k_scr, v_scr, acc_scr, m_scr, l_scr,
                    qp_scr, kp_scr, vp_scr, accp_scr, mp_scr, lp_scr):
    h = pl.program_id(1)
    j = pl.program_id(2)
    span = ATTN_SPAN
    group = span // GATHER_STRIDE
    cur_half = (j % 2) * span
    prev_half = span - cur_half

    @pl.when(j == 0)
    def _():
        zeros = jnp.zeros((span, HEAD_DIM), F32)
        k_scr[...] = zeros
        v_scr[...] = zeros
        kp_scr[span:2 * span, :] = zeros
        vp_scr[span:2 * span, :] = zeros
        for g in range(len(DILATED_BRANCHES)):
            bucket = bucket_ref[g]
            acc = jnp.full(bucket.shape, NEG_INF, F32)
            for b in range(REL_BUCKETS):
                acc = jnp.where(bucket == b, rb_ref[b, h], acc)
            bias_scr[g] = acc * LOG2_E

    for c in range(GATHER_STRIDE):
        rows = pl.ds(c, group, stride=GATHER_STRIDE)
        dst = pl.ds(pl.multiple_of(cur_half + c * group, group), group)
        qp_scr[c * group:(c + 1) * group, :] = q_ref[rows, :]
        kp_scr[dst, :] = k_ref[rows, :]
        vp_scr[dst, :] = v_ref[rows, :]

    col = lax.broadcasted_iota(jnp.int32, (Q_BLOCK, 2 * Q_BLOCK), 1)
    before_start = (j == 0) & (col < Q_BLOCK)
    ones = jnp.ones((2 * Q_BLOCK, HEAD_DIM), BF16)
    scale = HEAD_DIM ** -0.5 * LOG2_E

    def strided(start, size, stride):
        return pl.ds(start, size, stride=stride) if stride > 1 else pl.ds(start, size)

    def attend(g, first_block, q, k, v):
        s = _dot_nt((q * scale).astype(BF16), k.astype(BF16)) + bias_scr[g]
        if first_block:
            s = jnp.where(before_start, NEG_INF, s)
        m = jnp.max(s, axis=-1, keepdims=True)
        p = jnp.exp2(s - m).astype(BF16)
        pv = _dot(p, jnp.concatenate([v.astype(BF16), ones], axis=1))
        return pv[:, :HEAD_DIM], pv[:, HEAD_DIM:], jnp.broadcast_to(m, (Q_BLOCK, HEAD_DIM))

    for g, (_, d) in enumerate(DILATED_BRANCHES):
        for i in range(span // Q_BLOCK):
            r, n = i % d, i // d
            if d <= GATHER_STRIDE:
                start = n * Q_BLOCK * d + r
                rows_q = strided(start, Q_BLOCK, d)
                if n == 0:
                    prev = strided(span + start - Q_BLOCK * d, Q_BLOCK, d)
                    k = jnp.concatenate([k_scr[prev, :], k_ref[rows_q, :]], axis=0)
                    v = jnp.concatenate([v_scr[prev, :], v_ref[rows_q, :]], axis=0)
                else:
                    rows_kv = strided(start - Q_BLOCK * d, 2 * Q_BLOCK, d)
                    k, v = k_ref[rows_kv, :], v_ref[rows_kv, :]
                acc, l, m = attend(g, n == 0, q_ref[rows_q, :], k, v)
                acc_scr[g, rows_q, :] = acc
                l_scr[g, rows_q, :] = l
                m_scr[g, rows_q, :] = m
            else:
                inner = d // GATHER_STRIDE
                start = (r % GATHER_STRIDE) * group + n * Q_BLOCK * inner + r // GATHER_STRIDE
                rows_q = strided(start, Q_BLOCK, inner)
                if n == 0:
                    prev = strided(prev_half + start + group - Q_BLOCK * inner, Q_BLOCK, inner)
                else:
                    prev = strided(cur_half + start - Q_BLOCK * inner, Q_BLOCK, inner)
                cur = strided(cur_half + start, Q_BLOCK, inner)
                acc, l, m = attend(g, n == 0, qp_scr[rows_q, :],
                                   jnp.concatenate([kp_scr[prev, :], kp_scr[cur, :]], axis=0),
                                   jnp.concatenate([vp_scr[prev, :], vp_scr[cur, :]], axis=0))
                accp_scr[rows_q, :] = acc
                lp_scr[rows_q, :] = l
                mp_scr[rows_q, :] = m
        if d > GATHER_STRIDE:
            for c in range(GATHER_STRIDE):
                rows = pl.ds(c, group, stride=GATHER_STRIDE)
                acc_scr[g, rows, :] = accp_scr[c * group:(c + 1) * group, :]
                l_scr[g, rows, :] = lp_scr[c * group:(c + 1) * group, :]
                m_scr[g, rows, :] = mp_scr[c * group:(c + 1) * group, :]

    k_scr[...] = k_ref[...]
    v_scr[...] = v_ref[...]

    def merge(t, carry):
        rows = pl.ds(pl.multiple_of(t * Q_BLOCK, Q_BLOCK), Q_BLOCK)
        ms = [m_scr[g, rows, :] for g in range(len(DILATED_BRANCHES))]
        top = jnp.maximum(jnp.maximum(ms[0], ms[1]), ms[2])
        num = jnp.zeros((Q_BLOCK, HEAD_DIM), F32)
        den = jnp.zeros((Q_BLOCK, HEAD_DIM), F32)
        for g in range(len(DILATED_BRANCHES)):
            e = jnp.exp2(ms[g] - top)
            num = num + e * acc_scr[g, rows, :]
            den = den + e * l_scr[g, rows, :]
        o_ref[rows, :] = (num / den).astype(o_ref.dtype)
        return carry

    lax.fori_loop(0, span // Q_BLOCK, merge, 0)


def _dilated_attention(qkv, rel_bias, batch, seq):
    span = ATTN_SPAN
    nspan = seq // span
    nbr = len(DILATED_BRANCHES)
    buckets = jnp.asarray(_bucket_tables())

    def part(p):
        return pl.BlockSpec((span, HEAD_DIM), lambda b, h, j: (b * nspan + j, p * ATTN_HEADS + h))

    stat = pltpu.VMEM((nbr, span, HEAD_DIM), F32)
    one_span = pltpu.VMEM((span, HEAD_DIM), F32)
    two_spans = pltpu.VMEM((2 * span, HEAD_DIM), F32)
    return pl.pallas_call(
        _dilated_kernel,
        grid=(batch, ATTN_HEADS, nspan),
        in_specs=[pl.BlockSpec(memory_space=pltpu.SMEM),
                  pl.BlockSpec(buckets.shape, lambda b, h, j: (0, 0, 0)),
                  part(0), part(1), part(2)],
        out_specs=pl.BlockSpec((span, HEAD_DIM), lambda b, h, j: (b * nspan + j, h)),
        out_shape=jax.ShapeDtypeStruct((batch * seq, ATTN_WIDTH), BF16),
        scratch_shapes=[pltpu.VMEM((nbr, Q_BLOCK, 2 * Q_BLOCK), F32),
                        one_span, one_span, stat, stat, stat,
                        one_span, two_spans, two_spans, one_span, one_span, one_span],
        compiler_params=_params(("arbitrary", "arbitrary", "arbitrary")),
        name="dilated_attention",
    )(rel_bias, buckets, qkv, qkv, qkv)


def _hg_tables():
    c = HG_CHUNK
    t = np.arange(c)[:, None]
    u = np.arange(c)[None, :]
    mats = []
    for j in range(1, HG_LEVELS):
        s = 1 << j
        upper = (t & s) != 0
        q_side = (u >= (t & ~(s - 1))) & (u <= t)
        k_side = (u > t) & (u <= (t | (s - 1)))
        mats.append(np.where(upper, q_side, k_side))
    mats.append(u <= t)
    mats.append(u > t)
    stack = np.concatenate(mats, axis=0).astype(np.float32)
    level = np.where(u < t, np.floor(np.log2(np.maximum(t ^ u, 1))).astype(np.int32), -1)
    level = np.where(u == t, HG_LEVELS, level).astype(np.int32)
    return stack, level


def _hgrn_kernel(stack_ref, level_ref, lb_ref, gain_ref, fz_ref, iv_ref, qz_ref, gz_ref, o_ref, st_scr):
    @pl.when(pl.program_id(1) == 0)
    def _():
        st_scr[...] = jnp.zeros_like(st_scr)

    c = HG_CHUNK
    pairs = [slice(2 * p * HEAD_DIM, 2 * (p + 1) * HEAD_DIM) for p in range(HG_HEADS // 2)]
    halves = (slice(0, HEAD_DIM), slice(HEAD_DIM, 2 * HEAD_DIM))
    level = level_ref[...]

    f, kk, qq, ex = [], [], [], []
    for sl in pairs:
        lb = lb_ref[:, sl]
        f_p = lb + (1.0 - lb) * _sigmoid(fz_ref[:, sl])
        lf = jnp.log(f_p) * LOG2_E
        hi = lf.astype(BF16)
        lo = (lf - hi.astype(F32)).astype(BF16)
        ex.append(_dot(stack_ref[...], jnp.concatenate([hi, lo], axis=0)))
        qz = qz_ref[:, sl]
        f.append(f_p)
        kk.append(1.0 - f_p)
        qq.append(qz * _sigmoid(qz))

    odd = (lax.broadcasted_iota(jnp.int32, (c, 2 * HEAD_DIM), 0) & 1) != 0
    a = [jnp.zeros((c, c), F32) for _ in range(HG_HEADS)]
    for j in range(HG_LEVELS + 1):
        for p in range(len(pairs)):
            if j == 0:
                qy, ky = (qq[p] * jnp.where(odd, f[p], 1.0)).astype(BF16), kk[p].astype(BF16)
            elif j < HG_LEVELS:
                y = jnp.exp2(ex[p][(j - 1) * c:j * c])
                qy, ky = (qq[p] * y).astype(BF16), (kk[p] * y).astype(BF16)
            else:
                qy, ky = qq[p].astype(BF16), kk[p].astype(BF16)
            for i, half in enumerate(halves):
                h = 2 * p + i
                a[h] = jnp.where(level == j, _dot_nt(qy[:, half], ky[:, half]), a[h])

    for p, sl in enumerate(pairs):
        eb = jnp.exp2(ex[p][(HG_LEVELS - 1) * c:HG_LEVELS * c])
        er = jnp.exp2(ex[p][HG_LEVELS * c:(HG_LEVELS + 1) * c])
        qe = (qq[p] * eb).astype(BF16)
        ke = (kk[p] * er).astype(BF16)
        vv = iv_ref[:, sl].astype(BF16)
        outs = []
        for i, half in enumerate(halves):
            h = 2 * p + i
            st = st_scr[h]
            o = _dot_nt(qe[:, half], st.astype(BF16)) + _dot(a[h].astype(BF16), vv[:, half])
            st_scr[h] = st * eb[c - 1:c, half] + _dot_tn(vv[:, half], ke[:, half])
            outs.append(o * lax.rsqrt(jnp.mean(o * o, axis=-1, keepdims=True) + RMS_EPS))
        gz = gz_ref[:, sl]
        o = jnp.concatenate(outs, axis=1) * gain_ref[:, sl]
        o_ref[:, sl] = (o * (gz * _sigmoid(gz))).astype(o_ref.dtype)


def _hgrn2(proj, lb, gain, layer, batch, seq):
    stack, level = _hg_tables()
    stack2 = np.concatenate([stack, stack], axis=1)
    c = HG_CHUNK
    nchunk = seq // c
    first = 3 * ATTN_WIDTH // HG_WIDTH

    def part(g):
        return pl.BlockSpec((c, HG_WIDTH), lambda b, n: (b * nchunk + n, first + g))

    return pl.pallas_call(
        _hgrn_kernel,
        grid=(batch, nchunk),
        in_specs=[_const_spec(stack2.shape), _const_spec(level.shape),
                  _layer_spec((1, HG_WIDTH), layer), _layer_spec((1, HG_WIDTH), layer),
                  part(0), part(1), part(2), part(3)],
        out_specs=pl.BlockSpec((c, HG_WIDTH), lambda b, n: (b * nchunk + n, 0)),
        out_shape=jax.ShapeDtypeStruct((batch * seq, HG_WIDTH), BF16),
        scratch_shapes=[pltpu.VMEM((HG_HEADS, HEAD_DIM, HEAD_DIM), F32)],
        compiler_params=_params(("parallel", "arbitrary")),
        name="hgrn2",
    )(jnp.asarray(stack2, BF16), jnp.asarray(level), lb, gain, proj, proj, proj, proj)


def _mix_out_kernel(attn_ref, hg_ref, x_ref, g_ref, w_ref, out_ref):
    y = _dot(attn_ref[...], w_ref[:ATTN_WIDTH, :]) + _dot(hg_ref[...], w_ref[ATTN_WIDTH:, :])
    out_ref[...] = x_ref[...] + _rms(y, g_ref[...])


def _mix_out(attn, hg, x, gains, gi, w, layer, tm):
    m, d = x.shape
    row = lambda width: pl.BlockSpec((tm, width), lambda i: (i, 0))
    return pl.pallas_call(
        _mix_out_kernel,
        grid=(m // tm,),
        in_specs=[row(ATTN_WIDTH), row(HG_WIDTH), row(d), _layer_spec((1, d), gi),
                  _layer_spec(w.shape[1:], layer)],
        out_specs=row(d),
        out_shape=jax.ShapeDtypeStruct((m, d), F32),
        compiler_params=_params(("parallel",)),
        name="mix_out",
    )(attn, hg, x, gains, w)


def _cross_kernel(x_ref, gq_ref, go_ref, wq_ref, kv_ref, wo_ref, out_ref, co_scr):
    x = x_ref[...]
    d = x.shape[-1]
    hd = d // CROSS_HEADS
    hc = _rms(x, gq_ref[...]).astype(BF16)
    cq = _dot(hc, wq_ref[...]).astype(BF16)
    scale = hd ** -0.5 * LOG2_E
    for h in range(CROSS_HEADS):
        sl = slice(h * hd, (h + 1) * hd)
        s = _dot_nt(cq[:, sl], kv_ref[:, sl]) * scale
        p = jnp.exp2(s - jnp.max(s, axis=-1, keepdims=True))
        p = p * (1.0 / jnp.sum(p, axis=-1, keepdims=True))
        co_scr[:, sl] = _dot(p.astype(BF16), kv_ref[:, d + h * hd:d + (h + 1) * hd]).astype(BF16)
    y = _dot(co_scr[...], wo_ref[...])
    out_ref[...] = x + _rms(y, go_ref[...])


def _cross(x, gains, gq, go, wq, kv, wo, layer, seq, tm):
    m, d = x.shape
    n_mem = kv.shape[1]
    per_batch = seq // tm
    return pl.pallas_call(
        _cross_kernel,
        grid=(m // tm,),
        in_specs=[pl.BlockSpec((tm, d), lambda i: (i, 0)),
                  _layer_spec((1, d), gq), _layer_spec((1, d), go), _layer_spec(wq.shape[1:], layer),
                  pl.BlockSpec((None, n_mem, 2 * d), lambda i: (i // per_batch, 0, 0)),
                  _layer_spec(wo.shape[1:], layer)],
        out_specs=pl.BlockSpec((tm, d), lambda i: (i, 0)),
        out_shape=jax.ShapeDtypeStruct((m, d), F32),
        scratch_shapes=[pltpu.VMEM((tm, d), BF16)],
        compiler_params=_params(("parallel",)),
        name="cross_attention",
    )(x, gains, gains, wq, kv, wo)


def _ffn_kernel(x_ref, gi_ref, go_ref, wg_ref, wu_ref, wd_ref, out_ref, h_scr, acc_scr):
    j = pl.program_id(1)
    last = pl.num_programs(1) - 1
    tm = x_ref.shape[0]

    def step(first, final, parts):
        rows = tm // parts
        for r in range(parts):
            sl = slice(r * rows, (r + 1) * rows)
            if first:
                h = _rms(x_ref[sl, :], gi_ref[...]).astype(BF16)
                h_scr[sl, :] = h
            else:
                h = h_scr[sl, :]
            gate = _dot(h, wg_ref[...])
            up = _dot(h, wu_ref[...])
            act = (gate * _sigmoid(gate) * up).astype(BF16)
            acc = _dot(act, wd_ref[...])
            if not first:
                acc = acc_scr[sl, :] + acc
            if final:
                out_ref[sl, :] = x_ref[sl, :] + _rms(acc, go_ref[...])
            else:
                acc_scr[sl, :] = acc

    pl.when(j == 0)(lambda: step(True, False, 2))
    pl.when((j > 0) & (j < last))(lambda: step(False, False, 1))
    pl.when(j == last)(lambda: step(False, True, 2))


def _ffn(x, gains, gi, go, w_gate_up, w_down, layer, tm, tf):
    m, d = x.shape
    d_ff = w_down.shape[1]
    nf = d_ff // tf
    return pl.pallas_call(
        _ffn_kernel,
        grid=(m // tm, nf),
        in_specs=[pl.BlockSpec((tm, d), lambda i, j: (i, 0)),
                  _layer_spec((1, d), gi), _layer_spec((1, d), go),
                  pl.BlockSpec((None, d, tf), lambda i, j: (layer, 0, j)),
                  pl.BlockSpec((None, d, tf), lambda i, j: (layer, 0, nf + j)),
                  pl.BlockSpec((None, tf, d), lambda i, j: (layer, j, 0))],
        out_specs=pl.BlockSpec((tm, d), lambda i, j: (i, 0)),
        out_shape=jax.ShapeDtypeStruct((m, d), F32),
        scratch_shapes=[pltpu.VMEM((tm, d), BF16), pltpu.VMEM((tm, d), F32)],
        compiler_params=_params(("parallel", "arbitrary")),
        name="swiglu_ffn",
    )(x, gains, gains, w_gate_up, w_gate_up, w_down)


def kernel(x, mem, rel_bias, lb_logits, norm_gains, w_in, hg_norm, w_out, w_cq, w_ckv, w_co,
           w_gate_up, w_down):
    batch, seq, d = x.shape
    n_mem = mem.shape[1]
    depth = w_in.shape[0]
    n_norms = norm_gains.shape[1]
    lb_all = _lower_bounds(lb_logits).reshape(depth, 1, HG_WIDTH)
    hg_gain = hg_norm.astype(F32).reshape(depth, 1, HG_WIDTH)
    gains = norm_gains.astype(F32).reshape(depth * n_norms, 1, d)
    rel_bias = rel_bias.astype(F32)
    xf = x.reshape(batch * seq, d).astype(F32)
    memf = mem.reshape(batch * n_mem, d).astype(F32)
    w_in, w_out, w_cq, w_ckv, w_co, w_gate_up, w_down = (
        w.astype(BF16) for w in (w_in, w_out, w_cq, w_ckv, w_co, w_gate_up, w_down))

    for l in range(depth):
        gi = lambda i: l * n_norms + i
        proj = _norm_matmul(xf, gains, gi(0), w_in, l, F32, tm=1024, tn=1024, name="in_proj")
        attn = _dilated_attention(proj, rel_bias, batch, seq)
        hg = _hgrn2(proj, lb_all, hg_gain, l, batch, seq)
        xf = _mix_out(attn, hg, xf, gains, gi(1), w_out, l, tm=512)
        kv = _norm_matmul(memf, gains, gi(3), w_ckv, l, BF16, tm=batch * n_mem, tn=1024,
                          name="mem_kv_proj")
        xf = _cross(xf, gains, gi(2), gi(4), w_cq, kv.reshape(batch, n_mem, 2 * d), w_co, l, seq, tm=512)
        xf = _ffn(xf, gains, gi(5), gi(6), w_gate_up, w_down, l, tm=512, tf=512)
    return xf.reshape(batch, seq, d).astype(x.dtype)
```

```python
import math

import numpy as np
import jax
import jax.numpy as jnp
from jax import lax
from jax.experimental import pallas as pl
from jax.experimental.pallas import tpu as pltpu

F32 = jnp.float32
BF16 = jnp.bfloat16

HEAD_DIM = 128
ATTN_HEADS = 8
HG_HEADS = 8
ATTN_WIDTH = ATTN_HEADS * HEAD_DIM
HG_WIDTH = HG_HEADS * HEAD_DIM
Q_BLOCK = 128
DILATED_BRANCHES = ((128, 1), (512, 4), (2048, 16))
ATTN_SPAN = Q_BLOCK * max(d for _, d in DILATED_BRANCHES)
GATHER_STRIDE = 4
REL_BUCKETS = 32
REL_MAX_DIST = 2048
CROSS_HEADS = 4
RMS_EPS = 1e-6
NEG_INF = -1e30
HG_CHUNK = 128
HG_LEVELS = 7
HG_STEP_CHUNKS = 2
LOG2_E = 1.4426950408889634
VMEM_LIMIT = 52 * 1024 * 1024


def _rms(x, g):
    return x * lax.rsqrt(jnp.mean(x * x, axis=-1, keepdims=True) + RMS_EPS) * g


def _sigmoid(x):
    return 1.0 / (1.0 + jnp.exp(-x))


def _dot(a, b):
    return jnp.dot(a, b, preferred_element_type=F32)


def _dot_nt(a, b):
    return lax.dot_general(a, b, (((1,), (1,)), ((), ())), preferred_element_type=F32)


def _dot_tn(a, b):
    return lax.dot_general(a, b, (((0,), (0,)), ((), ())), preferred_element_type=F32)


def _params(sem):
    return pltpu.CompilerParams(dimension_semantics=sem, vmem_limit_bytes=VMEM_LIMIT)


def _const_spec(shape):
    nd = len(shape)
    return pl.BlockSpec(shape, lambda *_: (0,) * nd, pipeline_mode=pl.Buffered(1))


def _layer_spec(shape, layer):
    nd = len(shape)
    return pl.BlockSpec((None,) + tuple(shape), lambda *_: (layer,) + (0,) * nd,
                        pipeline_mode=pl.Buffered(1))


def _lb_kernel(z_ref, o_ref):
    z = z_ref[...]
    e = jnp.exp(z - jnp.max(z, axis=0, keepdims=True))
    p = e / jnp.sum(e, axis=0, keepdims=True)
    acc = jnp.zeros_like(p[0:1])
    for l in range(z.shape[0]):
        o_ref[l:l + 1, :] = acc
        acc = acc + p[l:l + 1]


def _lower_bounds(lb_logits):
    return pl.pallas_call(
        _lb_kernel, out_shape=jax.ShapeDtypeStruct(lb_logits.shape, F32),
        name="lower_bounds")(lb_logits.astype(F32))


def _norm_matmul_kernel(x_ref, g_ref, w_ref, o_ref, h_scr):
    j = pl.program_id(1)
    tm = x_ref.shape[0]

    @pl.when(j == 0)
    def _():
        rows = tm // 2
        for r in range(2):
            sl = slice(r * rows, (r + 1) * rows)
            h = _rms(x_ref[sl, :], g_ref[...]).astype(BF16)
            h_scr[sl, :] = h
            o_ref[sl, :] = _dot(h, w_ref[...]).astype(o_ref.dtype)

    @pl.when(j > 0)
    def _():
        o_ref[...] = _dot(h_scr[...], w_ref[...]).astype(o_ref.dtype)


def _column_blocks(w, tn):
    layers, d, n = w.shape
    return w.astype(BF16).reshape(layers, d, n // tn, tn).transpose(0, 2, 1, 3)


def _norm_matmul(x, gains, gi, w, layer, out_dtype, tm, name):
    m, d = x.shape
    nblk, tn = w.shape[1], w.shape[3]
    n = nblk * tn
    return pl.pallas_call(
        _norm_matmul_kernel,
        grid=(m // tm, nblk),
        in_specs=[pl.BlockSpec((tm, d), lambda i, j: (i, 0)),
                  _layer_spec((1, d), gi),
                  pl.BlockSpec((None, None, d, tn), lambda i, j: (layer, j, 0, 0))],
        out_specs=pl.BlockSpec((tm, tn), lambda i, j: (i, j)),
        out_shape=jax.ShapeDtypeStruct((m, n), out_dtype),
        scratch_shapes=[pltpu.VMEM((tm, d), BF16)],
        compiler_params=_params(("parallel", "arbitrary")),
        name=name,
    )(x, gains, w)


def _bucket_tables():
    qi = np.arange(Q_BLOCK)[:, None]
    kj = np.arange(2 * Q_BLOCK)[None, :]
    m = qi - kj + Q_BLOCK
    max_exact = REL_BUCKETS // 2
    tables = []
    for window, dilation in DILATED_BRANCHES:
        dist = np.maximum(m, 0) * dilation
        d_f = np.maximum(dist, 1).astype(np.float32)
        large = max_exact + (np.log(d_f / np.float32(max_exact))
                             / np.float32(math.log(REL_MAX_DIST / max_exact))
                             * (REL_BUCKETS - max_exact)).astype(np.int32)
        large = np.minimum(large, REL_BUCKETS - 1)
        bucket = np.where(dist < max_exact, dist, large)
        tables.append(np.where((m >= 0) & (m <= window // dilation), bucket, -1))
    return np.stack(tables).astype(np.int32)


def _dilated_kernel(rb_ref, bucket_ref, q_ref, k_ref, v_ref, o_ref,
                    bias_scr, k_scr, v_scr, acc_scr, m_scr, l_scr,
                    qp_scr, kp_scr, vp_scr, accp_scr, mp_scr, lp_scr):
    h = pl.program_id(1)
    j = pl.program_id(2)
    span = ATTN_SPAN
    group = span // GATHER_STRIDE
    cur_half = (j % 2) * span
    prev_half = span - cur_half

    @pl.when(j == 0)
    def _():
        zeros = jnp.zeros((span, HEAD_DIM), F32)
        k_scr[...] = zeros
        v_scr[...] = zeros
        kp_scr[span:2 * span, :] = zeros
        vp_scr[span:2 * span, :] = zeros
        for g in range(len(DILATED_BRANCHES)):
            bucket = bucket_ref[g]
            acc = jnp.full(bucket.shape, NEG_INF, F32)
            for b in range(REL_BUCKETS):
                acc = jnp.where(bucket == b, rb_ref[b, h], acc)
            bias_scr[g] = acc * LOG2_E

    for c in range(GATHER_STRIDE):
        rows = pl.ds(c, group, stride=GATHER_STRIDE)
        dst = pl.ds(pl.multiple_of(cur_half + c * group, group), group)
        qp_scr[c * group:(c + 1) * group, :] = q_ref[rows, :]
        kp_scr[dst, :] = k_ref[rows, :]
        vp_scr[dst, :] = v_ref[rows, :]

    col = lax.broadcasted_iota(jnp.int32, (Q_BLOCK, 2 * Q_BLOCK), 1)
    before_start = (j == 0) & (col < Q_BLOCK)
    ones = jnp.ones((2 * Q_BLOCK, HEAD_DIM), BF16)
    scale = HEAD_DIM ** -0.5 * LOG2_E

    def strided(start, size, stride):
        return pl.ds(start, size, stride=stride) if stride > 1 else pl.ds(start, size)

    def attend(g, first_block, q, k, v):
        s = _dot_nt((q * scale).astype(BF16), k.astype(BF16)) + bias_scr[g]
        if first_block:
            s = jnp.where(before_start, NEG_INF, s)
        m = jnp.max(s, axis=-1, keepdims=True)
        p = jnp.exp2(s - m).astype(BF16)
        pv = _dot(p, jnp.concatenate([v.astype(BF16), ones], axis=1))
        return pv[:, :HEAD_DIM], pv[:, HEAD_DIM:], jnp.broadcast_to(m, (Q_BLOCK, HEAD_DIM))

    for g, (_, d) in enumerate(DILATED_BRANCHES):
        for i in range(span // Q_BLOCK):
            r, n = i % d, i // d
            if d <= GATHER_STRIDE:
                start = n * Q_BLOCK * d + r
                rows_q = strided(start, Q_BLOCK, d)
                if n == 0:
                    prev = strided(span + start - Q_BLOCK * d, Q_BLOCK, d)
                    k = jnp.concatenate([k_scr[prev, :], k_ref[rows_q, :]], axis=0)
                    v = jnp.concatenate([v_scr[prev, :], v_ref[rows_q, :]], axis=0)
                else:
                    rows_kv = strided(start - Q_BLOCK * d, 2 * Q_BLOCK, d)
                    k, v = k_ref[rows_kv, :], v_ref[rows_kv, :]
                acc, l, m = attend(g, n == 0, q_ref[rows_q, :], k, v)
                acc_scr[g, rows_q, :] = acc
                l_scr[g, rows_q, :] = l
                m_scr[g, rows_q, :] = m
            else:
                inner = d // GATHER_STRIDE
                start = (r % GATHER_STRIDE) * group + n * Q_BLOCK * inner + r // GATHER_STRIDE
                rows_q = strided(start, Q_BLOCK, inner)
                if n == 0:
                    prev = strided(prev_half + start + group - Q_BLOCK * inner, Q_BLOCK, inner)
                else:
                    prev = strided(cur_half + start - Q_BLOCK * inner, Q_BLOCK, inner)
                cur = strided(cur_half + start, Q_BLOCK, inner)
                acc, l, m = attend(g, n == 0, qp_scr[rows_q, :],
                                   jnp.concatenate([kp_scr[prev, :], kp_scr[cur, :]], axis=0),
                                   jnp.concatenate([vp_scr[prev, :], vp_scr[cur, :]], axis=0))
                accp_scr[rows_q, :] = acc
                lp_scr[rows_q, :] = l
                mp_scr[rows_q, :] = m
        if d > GATHER_STRIDE:
            for c in range(GATHER_STRIDE):
                rows = pl.ds(c, group, stride=GATHER_STRIDE)
                acc_scr[g, rows, :] = accp_scr[c * group:(c + 1) * group, :]
                l_scr[g, rows, :] = lp_scr[c * group:(c + 1) * group, :]
                m_scr[g, rows, :] = mp_scr[c * group:(c + 1) * group, :]

    k_scr[...] = k_ref[...]
    v_scr[...] = v_ref[...]

    def merge(t, carry):
        rows = pl.ds(pl.multiple_of(t * Q_BLOCK, Q_BLOCK), Q_BLOCK)
        ms = [m_scr[g, rows, :] for g in range(len(DILATED_BRANCHES))]
        top = jnp.maximum(jnp.maximum(ms[0], ms[1]), ms[2])
        num = jnp.zeros((Q_BLOCK, HEAD_DIM), F32)
        den = jnp.zeros((Q_BLOCK, HEAD_DIM), F32)
        for g in range(len(DILATED_BRANCHES)):
            e = jnp.exp2(ms[g] - top)
            num = num + e * acc_scr[g, rows, :]
            den = den + e * l_scr[g, rows, :]
        o_ref[rows, :] = (num / den).astype(o_ref.dtype)
        return carry

    lax.fori_loop(0, span // Q_BLOCK, merge, 0)


def _dilated_attention(qkv, rel_bias, batch, seq):
    span = ATTN_SPAN
    nspan = seq // span
    nbr = len(DILATED_BRANCHES)
    buckets = jnp.asarray(_bucket_tables())

    def part(p):
        return pl.BlockSpec((span, HEAD_DIM), lambda b, h, j: (b * nspan + j, p * ATTN_HEADS + h))

    stat = pltpu.VMEM((nbr, span, HEAD_DIM), F32)
    one_span = pltpu.VMEM((span, HEAD_DIM), F32)
    two_spans = pltpu.VMEM((2 * span, HEAD_DIM), F32)
    return pl.pallas_call(
        _dilated_kernel,
        grid=(batch, ATTN_HEADS, nspan),
        in_specs=[pl.BlockSpec(memory_space=pltpu.SMEM),
                  pl.BlockSpec(buckets.shape, lambda b, h, j: (0, 0, 0)),
                  part(0), part(1), part(2)],
        out_specs=pl.BlockSpec((span, HEAD_DIM), lambda b, h, j: (b * nspan + j, h)),
        out_shape=jax.ShapeDtypeStruct((batch * seq, ATTN_WIDTH), BF16),
        scratch_shapes=[pltpu.VMEM((nbr, Q_BLOCK, 2 * Q_BLOCK), F32),
                        one_span, one_span, stat, stat, stat,
                        one_span, two_spans, two_spans, one_span, one_span, one_span],
        compiler_params=_params(("arbitrary", "arbitrary", "arbitrary")),
        name="dilated_attention",
    )(rel_bias, buckets, qkv, qkv, qkv)


def _hg_tables():
    c = HG_CHUNK
    t = np.arange(c)[:, None]
    u = np.arange(c)[None, :]
    mats = []
    for j in range(1, HG_LEVELS):
        s = 1 << j
        upper = (t & s) != 0
        q_side = (u >= (t & ~(s - 1))) & (u <= t)
        k_side = (u > t) & (u <= (t | (s - 1)))
        mats.append(np.where(upper, q_side, k_side))
    mats.append(u <= t)
    mats.append(u > t)
    stack = np.concatenate(mats, axis=0).astype(np.float32)
    level = np.where(u < t, np.floor(np.log2(np.maximum(t ^ u, 1))).astype(np.int32), -1)
    level = np.where(u == t, HG_LEVELS, level).astype(np.int32)
    return stack, level


def _hgrn_kernel(stack_ref, level_ref, lb_ref, gain_ref, fz_ref, iv_ref, qz_ref, gz_ref, o_ref, st_scr):
    @pl.when(pl.program_id(1) == 0)
    def _():
        st_scr[...] = jnp.zeros_like(st_scr)

    c = HG_CHUNK
    pairs = [slice(2 * p * HEAD_DIM, 2 * (p + 1) * HEAD_DIM) for p in range(HG_HEADS // 2)]
    halves = (slice(0, HEAD_DIM), slice(HEAD_DIM, 2 * HEAD_DIM))
    units = [(slice(ci * c, (ci + 1) * c), sl) for ci in range(HG_STEP_CHUNKS) for sl in pairs]
    level = level_ref[...]

    f, kk, qq, ex = [], [], [], []
    for rows, sl in units:
        lb = lb_ref[:, sl]
        f_u = lb + (1.0 - lb) * _sigmoid(fz_ref[rows, sl])
        lf = jnp.log(f_u) * LOG2_E
        hi = lf.astype(BF16)
        lo = (lf - hi.astype(F32)).astype(BF16)
        ex.append(_dot(stack_ref[...], jnp.concatenate([hi, lo], axis=0)))
        qz = qz_ref[rows, sl]
        f.append(f_u)
        kk.append(1.0 - f_u)
        qq.append(qz * _sigmoid(qz))

    odd = (lax.broadcasted_iota(jnp.int32, (c, 2 * HEAD_DIM), 0) & 1) != 0
    a = [[jnp.zeros((c, c), F32) for _ in halves] for _ in units]
    for j in range(HG_LEVELS + 1):
        for u in range(len(units)):
            if j == 0:
                qy, ky = (qq[u] * jnp.where(odd, f[u], 1.0)).astype(BF16), kk[u].astype(BF16)
            elif j < HG_LEVELS:
                y = jnp.exp2(ex[u][(j - 1) * c:j * c])
                qy, ky = (qq[u] * y).astype(BF16), (kk[u] * y).astype(BF16)
            else:
                qy, ky = qq[u].astype(BF16), kk[u].astype(BF16)
            for i, half in enumerate(halves):
                a[u][i] = jnp.where(level == j, _dot_nt(qy[:, half], ky[:, half]), a[u][i])

    for u, (rows, sl) in enumerate(units):
        p = u % len(pairs)
        eb = jnp.exp2(ex[u][(HG_LEVELS - 1) * c:HG_LEVELS * c])
        er = jnp.exp2(ex[u][HG_LEVELS * c:(HG_LEVELS + 1) * c])
        qe = (qq[u] * eb).astype(BF16)
        ke = (kk[u] * er).astype(BF16)
        vv = iv_ref[rows, sl].astype(BF16)
        outs = []
        for i, half in enumerate(halves):
            h = 2 * p + i
            st = st_scr[h]
            o = _dot_nt(qe[:, half], st.astype(BF16)) + _dot(a[u][i].astype(BF16), vv[:, half])
            st_scr[h] = st * eb[c - 1:c, half] + _dot_tn(vv[:, half], ke[:, half])
            outs.append(o * lax.rsqrt(jnp.mean(o * o, axis=-1, keepdims=True) + RMS_EPS))
        gz = gz_ref[rows, sl]
        o = jnp.concatenate(outs, axis=1) * gain_ref[:, sl]
        o_ref[rows, sl] = (o * (gz * _sigmoid(gz))).astype(o_ref.dtype)


def _hgrn2(proj, lb, gain, layer, batch, seq):
    stack, level = _hg_tables()
    stack2 = np.concatenate([stack, stack], axis=1)
    c = HG_CHUNK * HG_STEP_CHUNKS
    nchunk = seq // c
    first = 3 * ATTN_WIDTH // HG_WIDTH

    def part(g):
        return pl.BlockSpec((c, HG_WIDTH), lambda b, n: (b * nchunk + n, first + g))

    return pl.pallas_call(
        _hgrn_kernel,
        grid=(batch, nchunk),
        in_specs=[_const_spec(stack2.shape), _const_spec(level.shape),
                  _layer_spec((1, HG_WIDTH), layer), _layer_spec((1, HG_WIDTH), layer),
                  part(0), part(1), part(2), part(3)],
        out_specs=pl.BlockSpec((c, HG_WIDTH), lambda b, n: (b * nchunk + n, 0)),
        out_shape=jax.ShapeDtypeStruct((batch * seq, HG_WIDTH), BF16),
        scratch_shapes=[pltpu.VMEM((HG_HEADS, HEAD_DIM, HEAD_DIM), F32)],
        compiler_params=_params(("parallel", "arbitrary")),
        name="hgrn2",
    )(jnp.asarray(stack2, BF16), jnp.asarray(level), lb, gain, proj, proj, proj, proj)


def _mix_out_kernel(attn_ref, hg_ref, x_ref, g_ref, w_ref, out_ref):
    y = _dot(attn_ref[...], w_ref[:ATTN_WIDTH, :]) + _dot(hg_ref[...], w_ref[ATTN_WIDTH:, :])
    out_ref[...] = x_ref[...] + _rms(y, g_ref[...])


def _mix_out(attn, hg, x, gains, gi, w, layer, tm):
    m, d = x.shape
    row = lambda width: pl.BlockSpec((tm, width), lambda i: (i, 0))
    return pl.pallas_call(
        _mix_out_kernel,
        grid=(m // tm,),
        in_specs=[row(ATTN_WIDTH), row(HG_WIDTH), row(d), _layer_spec((1, d), gi),
                  _layer_spec(w.shape[1:], layer)],
        out_specs=row(d),
        out_shape=jax.ShapeDtypeStruct((m, d), F32),
        compiler_params=_params(("parallel",)),
        name="mix_out",
    )(attn, hg, x, gains, w)


def _cross_kernel(x_ref, gq_ref, go_ref, wq_ref, kv_ref, wo_ref, out_ref, co_scr):
    x = x_ref[...]
    d = x.shape[-1]
    hd = d // CROSS_HEADS
    hc = _rms(x, gq_ref[...]).astype(BF16)
    cq = _dot(hc, wq_ref[...]).astype(BF16)
    scale = hd ** -0.5 * LOG2_E
    for h in range(CROSS_HEADS):
        sl = slice(h * hd, (h + 1) * hd)
        s = _dot_nt(cq[:, sl], kv_ref[:, sl]) * scale
        p = jnp.exp2(s - jnp.max(s, axis=-1, keepdims=True))
        p = p * (1.0 / jnp.sum(p, axis=-1, keepdims=True))
        co_scr[:, sl] = _dot(p.astype(BF16), kv_ref[:, d + h * hd:d + (h + 1) * hd]).astype(BF16)
    y = _dot(co_scr[...], wo_ref[...])
    out_ref[...] = x + _rms(y, go_ref[...])


def _cross(x, gains, gq, go, wq, kv, wo, layer, seq, tm):
    m, d = x.shape
    n_mem = kv.shape[1]
    per_batch = seq // tm
    return pl.pallas_call(
        _cross_kernel,
        grid=(m // tm,),
        in_specs=[pl.BlockSpec((tm, d), lambda i: (i, 0)),
                  _layer_spec((1, d), gq), _layer_spec((1, d), go), _layer_spec(wq.shape[1:], layer),
                  pl.BlockSpec((None, n_mem, 2 * d), lambda i: (i // per_batch, 0, 0)),
                  _layer_spec(wo.shape[1:], layer)],
        out_specs=pl.BlockSpec((tm, d), lambda i: (i, 0)),
        out_shape=jax.ShapeDtypeStruct((m, d), F32),
        scratch_shapes=[pltpu.VMEM((tm, d), BF16)],
        compiler_params=_params(("parallel",)),
        name="cross_attention",
    )(x, gains, gains, wq, kv, wo)


def _ffn_kernel(x_ref, gi_ref, go_ref, wg_ref, wu_ref, wd_ref, out_ref, h_scr, acc_scr):
    j = pl.program_id(1)
    last = pl.num_programs(1) - 1
    tm = x_ref.shape[0]

    def step(first, final, parts):
        rows = tm // parts
        for r in range(parts):
            sl = slice(r * rows, (r + 1) * rows)
            if first:
                h = _rms(x_ref[sl, :], gi_ref[...]).astype(BF16)
                h_scr[sl, :] = h
            else:
                h = h_scr[sl, :]
            gate = _dot(h, wg_ref[...])
            up = _dot(h, wu_ref[...])
            act = (gate * _sigmoid(gate) * up).astype(BF16)
            acc = _dot(act, wd_ref[...])
            if not first:
                acc = acc_scr[sl, :] + acc
            if final:
                out_ref[sl, :] = x_ref[sl, :] + _rms(acc, go_ref[...])
            else:
                acc_scr[sl, :] = acc

    pl.when(j == 0)(lambda: step(True, False, 2))
    pl.when((j > 0) & (j < last))(lambda: step(False, False, 1))
    pl.when(j == last)(lambda: step(False, True, 2))


def _ffn(x, gains, gi, go, w_gate_up, w_down, layer, tm):
    m, d = x.shape
    nf, tf = w_gate_up.shape[1] // 2, w_gate_up.shape[3]
    return pl.pallas_call(
        _ffn_kernel,
        grid=(m // tm, nf),
        in_specs=[pl.BlockSpec((tm, d), lambda i, j: (i, 0)),
                  _layer_spec((1, d), gi), _layer_spec((1, d), go),
                  pl.BlockSpec((None, None, d, tf), lambda i, j: (layer, j, 0, 0)),
                  pl.BlockSpec((None, None, d, tf), lambda i, j: (layer, nf + j, 0, 0)),
                  pl.BlockSpec((None, tf, d), lambda i, j: (layer, j, 0))],
        out_specs=pl.BlockSpec((tm, d), lambda i, j: (i, 0)),
        out_shape=jax.ShapeDtypeStruct((m, d), F32),
        scratch_shapes=[pltpu.VMEM((tm, d), BF16), pltpu.VMEM((tm, d), F32)],
        compiler_params=_params(("parallel", "arbitrary")),
        name="swiglu_ffn",
    )(x, gains, gains, w_gate_up, w_gate_up, w_down)


def kernel(x, mem, rel_bias, lb_logits, norm_gains, w_in, hg_norm, w_out, w_cq, w_ckv, w_co,
           w_gate_up, w_down):
    batch, seq, d = x.shape
    n_mem = mem.shape[1]
    depth = w_in.shape[0]
    n_norms = norm_gains.shape[1]
    lb_all = _lower_bounds(lb_logits).reshape(depth, 1, HG_WIDTH)
    hg_gain = hg_norm.astype(F32).reshape(depth, 1, HG_WIDTH)
    gains = norm_gains.astype(F32).reshape(depth * n_norms, 1, d)
    rel_bias = rel_bias.astype(F32)
    xf = x.reshape(batch * seq, d).astype(F32)
    memf = mem.reshape(batch * n_mem, d).astype(F32)
    w_out, w_cq, w_co, w_down = (w.astype(BF16) for w in (w_out, w_cq, w_co, w_down))
    w_in, w_ckv, w_gate_up = (_column_blocks(w, tn) for w, tn in ((w_in, 1024), (w_ckv, 1024), (w_gate_up, 512)))

    for l in range(depth):
        gi = lambda i: l * n_norms + i
        proj = _norm_matmul(xf, gains, gi(0), w_in, l, F32, tm=1024, name="in_proj")
        attn = _dilated_attention(proj, rel_bias, batch, seq)
        hg = _hgrn2(proj, lb_all, hg_gain, l, batch, seq)
        xf = _mix_out(attn, hg, xf, gains, gi(1), w_out, l, tm=512)
        kv = _norm_matmul(memf, gains, gi(3), w_ckv, l, BF16, tm=batch * n_mem, name="mem_kv_proj")
        xf = _cross(xf, gains, gi(2), gi(4), w_cq, kv.reshape(batch, n_mem, 2 * d), w_co, l, seq, tm=512)
        xf = _ffn(xf, gains, gi(5), gi(6), w_gate_up, w_down, l, tm=512)
    return xf.reshape(batch, seq, d).astype(x.dtype)
```

```python
import functools
import math

import numpy as np
import jax
import jax.numpy as jnp
from jax import lax
from jax.experimental import pallas as pl
from jax.experimental.pallas import tpu as pltpu

F32 = jnp.float32
BF16 = jnp.bfloat16

HEAD_DIM = 128
ATTN_HEADS = 8
HG_HEADS = 8
ATTN_WIDTH = ATTN_HEADS * HEAD_DIM
HG_WIDTH = HG_HEADS * HEAD_DIM
Q_BLOCK = 128
DILATED_BRANCHES = ((128, 1), (512, 4), (2048, 16))
ATTN_SPAN = Q_BLOCK * max(d for _, d in DILATED_BRANCHES)
GATHER_STRIDE = 4
REL_BUCKETS = 32
REL_MAX_DIST = 2048
CROSS_HEADS = 4
RMS_EPS = 1e-6
NEG_INF = -1e30
HG_CHUNK = 128
HG_LEVELS = 7
HG_STEP_CHUNKS = 2
LOG2_E = 1.4426950408889634
VMEM_LIMIT = 52 * 1024 * 1024
FFN_VMEM_LIMIT = 58 * 1024 * 1024
FFN_ROW_PARTS = 2


def _rms(x, g):
    return x * lax.rsqrt(jnp.mean(x * x, axis=-1, keepdims=True) + RMS_EPS) * g


def _sigmoid(x):
    return 1.0 / (1.0 + jnp.exp(-x))


def _dot(a, b):
    return jnp.dot(a, b, preferred_element_type=F32)


def _dot_nt(a, b):
    return lax.dot_general(a, b, (((1,), (1,)), ((), ())), preferred_element_type=F32)


def _dot_tn(a, b):
    return lax.dot_general(a, b, (((0,), (0,)), ((), ())), preferred_element_type=F32)


def _params(sem):
    return pltpu.CompilerParams(dimension_semantics=sem, vmem_limit_bytes=VMEM_LIMIT)


def _const_spec(shape):
    nd = len(shape)
    return pl.BlockSpec(shape, lambda *_: (0,) * nd, pipeline_mode=pl.Buffered(1))


def _layer_spec(shape, layer):
    nd = len(shape)
    return pl.BlockSpec((None,) + tuple(shape), lambda *_: (layer,) + (0,) * nd,
                        pipeline_mode=pl.Buffered(1))


def _lb_kernel(z_ref, o_ref):
    z = z_ref[...]
    e = jnp.exp(z - jnp.max(z, axis=0, keepdims=True))
    p = e / jnp.sum(e, axis=0, keepdims=True)
    acc = jnp.zeros_like(p[0:1])
    for l in range(z.shape[0]):
        o_ref[l:l + 1, :] = acc
        acc = acc + p[l:l + 1]


def _lower_bounds(lb_logits):
    return pl.pallas_call(
        _lb_kernel, out_shape=jax.ShapeDtypeStruct(lb_logits.shape, F32),
        name="lower_bounds")(lb_logits.astype(F32))


def _norm_matmul_kernel(x_ref, g_ref, w_ref, *refs, splits):
    o_refs, h_scr = refs[:-1], refs[-1]
    j = pl.program_id(1)
    tm = x_ref.shape[0]

    @pl.when(j == 0)
    def _():
        rows = tm // 2
        for r in range(2):
            sl = slice(r * rows, (r + 1) * rows)
            h = _rms(x_ref[sl, :], g_ref[...]).astype(BF16)
            h_scr[sl, :] = h
            o_refs[0][sl, :] = _dot(h, w_ref[...]).astype(o_refs[0].dtype)

    start = 0
    for o_ref, count in zip(o_refs, splits):
        lo, hi = max(start, 1), start + count
        start = hi
        if hi > lo:
            @pl.when((j >= lo) & (j < hi))
            def _(o_ref=o_ref):
                o_ref[...] = _dot(h_scr[...], w_ref[...]).astype(o_ref.dtype)


def _norm_matmul(x, gains, gi, w, layer, outs, tm, tn, name):
    m, d = x.shape
    n = w.shape[2]
    splits = tuple(cols // tn for cols, _ in outs)
    assert sum(splits) * tn == n

    def out_spec(first, count):
        return pl.BlockSpec((tm, tn), lambda i, j: (i, jnp.clip(j - first, 0, count - 1)))

    firsts = [sum(splits[:k]) for k in range(len(splits))]
    return pl.pallas_call(
        functools.partial(_norm_matmul_kernel, splits=splits),
        grid=(m // tm, n // tn),
        in_specs=[pl.BlockSpec((tm, d), lambda i, j: (i, 0)),
                  _layer_spec((1, d), gi),
                  pl.BlockSpec((None, d, tn), lambda i, j: (layer, 0, j))],
        out_specs=[out_spec(first, count) for first, count in zip(firsts, splits)],
        out_shape=[jax.ShapeDtypeStruct((m, cols), dtype) for cols, dtype in outs],
        scratch_shapes=[pltpu.VMEM((tm, d), BF16)],
        compiler_params=_params(("parallel", "arbitrary")),
        name=name,
    )(x, gains, w)


def _bucket_tables():
    qi = np.arange(Q_BLOCK)[:, None]
    kj = np.arange(2 * Q_BLOCK)[None, :]
    m = qi - kj + Q_BLOCK
    max_exact = REL_BUCKETS // 2
    tables = []
    for window, dilation in DILATED_BRANCHES:
        dist = np.maximum(m, 0) * dilation
        d_f = np.maximum(dist, 1).astype(np.float32)
        large = max_exact + (np.log(d_f / np.float32(max_exact))
                             / np.float32(math.log(REL_MAX_DIST / max_exact))
                             * (REL_BUCKETS - max_exact)).astype(np.int32)
        large = np.minimum(large, REL_BUCKETS - 1)
        bucket = np.where(dist < max_exact, dist, large)
        tables.append(np.where((m >= 0) & (m <= window // dilation), bucket, -1))
    return np.stack(tables).astype(np.int32)


def _dilated_kernel(rb_ref, bucket_ref, q_ref, k_ref, v_ref, o_ref,
                    bias_scr, k_scr, v_scr, acc_scr, m_scr, l_scr,
                    qp_scr, kp_scr, vp_scr, accp_scr, mp_scr, lp_scr):
    h = pl.program_id(1)
    j = pl.program_id(2)
    span = ATTN_SPAN
    group = span // GATHER_STRIDE
    cur_half = (j % 2) * span
    prev_half = span - cur_half

    @pl.when(j == 0)
    def _():
        zeros = jnp.zeros((span, HEAD_DIM), F32)
        k_scr[...] = zeros
        v_scr[...] = zeros
        kp_scr[span:2 * span, :] = zeros
        vp_scr[span:2 * span, :] = zeros
        for g in range(len(DILATED_BRANCHES)):
            bucket = bucket_ref[g]
            acc = jnp.full(bucket.shape, NEG_INF, F32)
            for b in range(REL_BUCKETS):
                acc = jnp.where(bucket == b, rb_ref[b, h], acc)
            bias_scr[g] = acc * LOG2_E

    for c in range(GATHER_STRIDE):
        rows = pl.ds(c, group, stride=GATHER_STRIDE)
        dst = pl.ds(pl.multiple_of(cur_half + c * group, group), group)
        qp_scr[c * group:(c + 1) * group, :] = q_ref[rows, :]
        kp_scr[dst, :] = k_ref[rows, :]
        vp_scr[dst, :] = v_ref[rows, :]

    col = lax.broadcasted_iota(jnp.int32, (Q_BLOCK, 2 * Q_BLOCK), 1)
    before_start = (j == 0) & (col < Q_BLOCK)
    ones = jnp.ones((2 * Q_BLOCK, HEAD_DIM), BF16)
    scale = HEAD_DIM ** -0.5 * LOG2_E

    def strided(start, size, stride):
        return pl.ds(start, size, stride=stride) if stride > 1 else pl.ds(start, size)

    def attend(g, first_block, q, k, v):
        s = _dot_nt((q * scale).astype(BF16), k.astype(BF16)) + bias_scr[g]
        if first_block:
            s = jnp.where(before_start, NEG_INF, s)
        m = jnp.max(s, axis=-1, keepdims=True)
        p = jnp.exp2(s - m).astype(BF16)
        pv = _dot(p, jnp.concatenate([v.astype(BF16), ones], axis=1))
        return pv[:, :HEAD_DIM], pv[:, HEAD_DIM:], jnp.broadcast_to(m, (Q_BLOCK, HEAD_DIM))

    for g, (_, d) in enumerate(DILATED_BRANCHES):
        for i in range(span // Q_BLOCK):
            r, n = i % d, i // d
            if d <= GATHER_STRIDE:
                start = n * Q_BLOCK * d + r
                rows_q = strided(start, Q_BLOCK, d)
                if n == 0:
                    prev = strided(span + start - Q_BLOCK * d, Q_BLOCK, d)
                    k = jnp.concatenate([k_scr[prev, :], k_ref[rows_q, :]], axis=0)
                    v = jnp.concatenate([v_scr[prev, :], v_ref[rows_q, :]], axis=0)
                else:
                    rows_kv = strided(start - Q_BLOCK * d, 2 * Q_BLOCK, d)
                    k, v = k_ref[rows_kv, :], v_ref[rows_kv, :]
                acc, l, m = attend(g, n == 0, q_ref[rows_q, :], k, v)
                acc_scr[g, rows_q, :] = acc
                l_scr[g, rows_q, :] = l
                m_scr[g, rows_q, :] = m
            else:
                inner = d // GATHER_STRIDE
                start = (r % GATHER_STRIDE) * group + n * Q_BLOCK * inner + r // GATHER_STRIDE
                rows_q = strided(start, Q_BLOCK, inner)
                if n == 0:
                    prev = strided(prev_half + start + group - Q_BLOCK * inner, Q_BLOCK, inner)
                else:
                    prev = strided(cur_half + start - Q_BLOCK * inner, Q_BLOCK, inner)
                cur = strided(cur_half + start, Q_BLOCK, inner)
                acc, l, m = attend(g, n == 0, qp_scr[rows_q, :],
                                   jnp.concatenate([kp_scr[prev, :], kp_scr[cur, :]], axis=0),
                                   jnp.concatenate([vp_scr[prev, :], vp_scr[cur, :]], axis=0))
                accp_scr[rows_q, :] = acc
                lp_scr[rows_q, :] = l
                mp_scr[rows_q, :] = m
        if d > GATHER_STRIDE:
            for c in range(GATHER_STRIDE):
                rows = pl.ds(c, group, stride=GATHER_STRIDE)
                acc_scr[g, rows, :] = accp_scr[c * group:(c + 1) * group, :]
                l_scr[g, rows, :] = lp_scr[c * group:(c + 1) * group, :]
                m_scr[g, rows, :] = mp_scr[c * group:(c + 1) * group, :]

    k_scr[...] = k_ref[...]
    v_scr[...] = v_ref[...]

    def merge(t, carry):
        rows = pl.ds(pl.multiple_of(t * Q_BLOCK, Q_BLOCK), Q_BLOCK)
        ms = [m_scr[g, rows, :] for g in range(len(DILATED_BRANCHES))]
        top = jnp.maximum(jnp.maximum(ms[0], ms[1]), ms[2])
        num = jnp.zeros((Q_BLOCK, HEAD_DIM), F32)
        den = jnp.zeros((Q_BLOCK, HEAD_DIM), F32)
        for g in range(len(DILATED_BRANCHES)):
            e = jnp.exp2(ms[g] - top)
            num = num + e * acc_scr[g, rows, :]
            den = den + e * l_scr[g, rows, :]
        o_ref[rows, :] = (num / den).astype(o_ref.dtype)
        return carry

    lax.fori_loop(0, span // Q_BLOCK, merge, 0)


def _dilated_attention(qkv, rel_bias, batch, seq):
    span = ATTN_SPAN
    nspan = seq // span
    nbr = len(DILATED_BRANCHES)
    buckets = jnp.asarray(_bucket_tables())

    def part(p):
        return pl.BlockSpec((span, HEAD_DIM), lambda b, h, j: (b * nspan + j, p * ATTN_HEADS + h))

    stat = pltpu.VMEM((nbr, span, HEAD_DIM), F32)
    one_span = pltpu.VMEM((span, HEAD_DIM), F32)
    two_spans = pltpu.VMEM((2 * span, HEAD_DIM), F32)
    return pl.pallas_call(
        _dilated_kernel,
        grid=(batch, ATTN_HEADS, nspan),
        in_specs=[pl.BlockSpec(memory_space=pltpu.SMEM),
                  pl.BlockSpec(buckets.shape, lambda b, h, j: (0, 0, 0)),
                  part(0), part(1), part(2)],
        out_specs=pl.BlockSpec((span, HEAD_DIM), lambda b, h, j: (b * nspan + j, h)),
        out_shape=jax.ShapeDtypeStruct((batch * seq, ATTN_WIDTH), BF16),
        scratch_shapes=[pltpu.VMEM((nbr, Q_BLOCK, 2 * Q_BLOCK), F32),
                        one_span, one_span, stat, stat, stat,
                        one_span, two_spans, two_spans, one_span, one_span, one_span],
        compiler_params=_params(("arbitrary", "arbitrary", "arbitrary")),
        name="dilated_attention",
    )(rel_bias, buckets, qkv, qkv, qkv)


def _hg_tables():
    c = HG_CHUNK
    t = np.arange(c)[:, None]
    u = np.arange(c)[None, :]
    mats = []
    for j in range(1, HG_LEVELS):
        s = 1 << j
        upper = (t & s) != 0
        q_side = (u >= (t & ~(s - 1))) & (u <= t)
        k_side = (u > t) & (u <= (t | (s - 1)))
        mats.append(np.where(upper, q_side, k_side))
    mats.append(u <= t)
    mats.append(u > t)
    stack = np.concatenate(mats, axis=0).astype(np.float32)
    level = np.where(u < t, np.floor(np.log2(np.maximum(t ^ u, 1))).astype(np.int32), -1)
    level = np.where(u == t, HG_LEVELS, level).astype(np.int32)
    return stack, level


def _hgrn_kernel(stack_ref, level_ref, lb_ref, gain_ref, fz_ref, iv_ref, qz_ref, gz_ref, o_ref, st_scr):
    @pl.when(pl.program_id(1) == 0)
    def _():
        st_scr[...] = jnp.zeros_like(st_scr)

    c = HG_CHUNK
    pairs = [slice(2 * p * HEAD_DIM, 2 * (p + 1) * HEAD_DIM) for p in range(HG_HEADS // 2)]
    halves = (slice(0, HEAD_DIM), slice(HEAD_DIM, 2 * HEAD_DIM))
    units = [(slice(ci * c, (ci + 1) * c), sl) for ci in range(HG_STEP_CHUNKS) for sl in pairs]
    level = level_ref[...]

    f, kk, qq, ex = [], [], [], []
    for rows, sl in units:
        lb = lb_ref[:, sl]
        f_u = lb + (1.0 - lb) * _sigmoid(fz_ref[rows, sl].astype(F32))
        lf = jnp.log(f_u) * LOG2_E
        hi = lf.astype(BF16)
        lo = (lf - hi.astype(F32)).astype(BF16)
        ex.append(_dot(stack_ref[...], jnp.concatenate([hi, lo], axis=0)))
        qz = qz_ref[rows, sl].astype(F32)
        f.append(f_u)
        kk.append(1.0 - f_u)
        qq.append(qz * _sigmoid(qz))

    odd = (lax.broadcasted_iota(jnp.int32, (c, 2 * HEAD_DIM), 0) & 1) != 0
    a = [[jnp.zeros((c, c), F32) for _ in halves] for _ in units]
    for j in range(HG_LEVELS + 1):
        for u in range(len(units)):
            if j == 0:
                qy, ky = (qq[u] * jnp.where(odd, f[u], 1.0)).astype(BF16), kk[u].astype(BF16)
            elif j < HG_LEVELS:
                y = jnp.exp2(ex[u][(j - 1) * c:j * c])
                qy, ky = (qq[u] * y).astype(BF16), (kk[u] * y).astype(BF16)
            else:
                qy, ky = qq[u].astype(BF16), kk[u].astype(BF16)
            for i, half in enumerate(halves):
                a[u][i] = jnp.where(level == j, _dot_nt(qy[:, half], ky[:, half]), a[u][i])

    for u, (rows, sl) in enumerate(units):
        p = u % len(pairs)
        eb = jnp.exp2(ex[u][(HG_LEVELS - 1) * c:HG_LEVELS * c])
        er = jnp.exp2(ex[u][HG_LEVELS * c:(HG_LEVELS + 1) * c])
        qe = (qq[u] * eb).astype(BF16)
        ke = (kk[u] * er).astype(BF16)
        vv = iv_ref[rows, sl].astype(BF16)
        outs = []
        for i, half in enumerate(halves):
            h = 2 * p + i
            st = st_scr[h]
            o = _dot_nt(qe[:, half], st.astype(BF16)) + _dot(a[u][i].astype(BF16), vv[:, half])
            st_scr[h] = st * eb[c - 1:c, half] + _dot_tn(vv[:, half], ke[:, half])
            outs.append(o * lax.rsqrt(jnp.mean(o * o, axis=-1, keepdims=True) + RMS_EPS))
        gz = gz_ref[rows, sl].astype(F32)
        o = jnp.concatenate(outs, axis=1) * gain_ref[:, sl]
        o_ref[rows, sl] = (o * (gz * _sigmoid(gz))).astype(o_ref.dtype)


def _hgrn2(proj, lb, gain, layer, batch, seq):
    stack, level = _hg_tables()
    stack2 = np.concatenate([stack, stack], axis=1)
    c = HG_CHUNK * HG_STEP_CHUNKS
    nchunk = seq // c

    def part(g):
        return pl.BlockSpec((c, HG_WIDTH), lambda b, n: (b * nchunk + n, g))

    return pl.pallas_call(
        _hgrn_kernel,
        grid=(batch, nchunk),
        in_specs=[_const_spec(stack2.shape), _const_spec(level.shape),
                  _layer_spec((1, HG_WIDTH), layer), _layer_spec((1, HG_WIDTH), layer),
                  part(0), part(1), part(2), part(3)],
        out_specs=pl.BlockSpec((c, HG_WIDTH), lambda b, n: (b * nchunk + n, 0)),
        out_shape=jax.ShapeDtypeStruct((batch * seq, HG_WIDTH), BF16),
        scratch_shapes=[pltpu.VMEM((HG_HEADS, HEAD_DIM, HEAD_DIM), F32)],
        compiler_params=_params(("parallel", "arbitrary")),
        name="hgrn2",
    )(jnp.asarray(stack2, BF16), jnp.asarray(level), lb, gain, proj, proj, proj, proj)


def _mix_out_kernel(attn_ref, hg_ref, x_ref, g_ref, w_ref, out_ref):
    y = _dot(attn_ref[...], w_ref[:ATTN_WIDTH, :]) + _dot(hg_ref[...], w_ref[ATTN_WIDTH:, :])
    out_ref[...] = x_ref[...] + _rms(y, g_ref[...])


def _mix_out(attn, hg, x, gains, gi, w, layer, tm):
    m, d = x.shape
    row = lambda width: pl.BlockSpec((tm, width), lambda i: (i, 0))
    return pl.pallas_call(
        _mix_out_kernel,
        grid=(m // tm,),
        in_specs=[row(ATTN_WIDTH), row(HG_WIDTH), row(d), _layer_spec((1, d), gi),
                  _layer_spec(w.shape[1:], layer)],
        out_specs=row(d),
        out_shape=jax.ShapeDtypeStruct((m, d), F32),
        compiler_params=_params(("parallel",)),
        name="mix_out",
    )(attn, hg, x, gains, w)


def _cross_kernel(x_ref, gq_ref, go_ref, wq_ref, kv_ref, wo_ref, out_ref, co_scr):
    x = x_ref[...]
    d = x.shape[-1]
    hd = d // CROSS_HEADS
    hc = _rms(x, gq_ref[...]).astype(BF16)
    cq = _dot(hc, wq_ref[...]).astype(BF16)
    scale = hd ** -0.5 * LOG2_E
    for h in range(CROSS_HEADS):
        sl = slice(h * hd, (h + 1) * hd)
        s = _dot_nt(cq[:, sl], kv_ref[:, sl]) * scale
        p = jnp.exp2(s - jnp.max(s, axis=-1, keepdims=True))
        p = p * (1.0 / jnp.sum(p, axis=-1, keepdims=True))
        co_scr[:, sl] = _dot(p.astype(BF16), kv_ref[:, d + h * hd:d + (h + 1) * hd]).astype(BF16)
    y = _dot(co_scr[...], wo_ref[...])
    out_ref[...] = x + _rms(y, go_ref[...])


def _cross(x, gains, gq, go, wq, kv, wo, layer, seq, tm):
    m, d = x.shape
    n_mem = kv.shape[1]
    per_batch = seq // tm
    return pl.pallas_call(
        _cross_kernel,
        grid=(m // tm,),
        in_specs=[pl.BlockSpec((tm, d), lambda i: (i, 0)),
                  _layer_spec((1, d), gq), _layer_spec((1, d), go), _layer_spec(wq.shape[1:], layer),
                  pl.BlockSpec((None, n_mem, 2 * d), lambda i: (i // per_batch, 0, 0)),
                  _layer_spec(wo.shape[1:], layer)],
        out_specs=pl.BlockSpec((tm, d), lambda i: (i, 0)),
        out_shape=jax.ShapeDtypeStruct((m, d), F32),
        scratch_shapes=[pltpu.VMEM((tm, d), BF16)],
        compiler_params=_params(("parallel",)),
        name="cross_attention",
    )(x, gains, gains, wq, kv, wo)


def _ffn_kernel(x_ref, gi_ref, go_ref, wg_ref, wu_ref, wd_ref, out_ref, h_scr):
    j = pl.program_id(1)
    last = pl.num_programs(1) - 1
    rows = x_ref.shape[0] // FFN_ROW_PARTS

    def step(first, final):
        for r in range(FFN_ROW_PARTS):
            sl = slice(r * rows, (r + 1) * rows)
            if first:
                h = _rms(x_ref[sl, :], gi_ref[...]).astype(BF16)
                h_scr[sl, :] = h
            else:
                h = h_scr[sl, :]
            gate = _dot(h, wg_ref[...])
            up = _dot(h, wu_ref[...])
            act = (gate * _sigmoid(gate) * up).astype(BF16)
            acc = _dot(act, wd_ref[...])
            if not first:
                acc = out_ref[sl, :] + acc
            if final:
                acc = x_ref[sl, :] + _rms(acc, go_ref[...])
            out_ref[sl, :] = acc

    pl.when(j == 0)(lambda: step(True, False))
    pl.when((j > 0) & (j < last))(lambda: step(False, False))
    pl.when(j == last)(lambda: step(False, True))


def _ffn(x, gains, gi, go, w_gate_up, w_down, layer, tm, tf):
    m, d = x.shape
    nf = w_down.shape[1] // tf
    return pl.pallas_call(
        _ffn_kernel,
        grid=(m // tm, nf),
        in_specs=[pl.BlockSpec((tm, d), lambda i, j: (i, 0)),
                  _layer_spec((1, d), gi), _layer_spec((1, d), go),
                  pl.BlockSpec((None, d, tf), lambda i, j: (layer, 0, j)),
                  pl.BlockSpec((None, d, tf), lambda i, j: (layer, 0, nf + j)),
                  pl.BlockSpec((None, tf, d), lambda i, j: (layer, j, 0))],
        out_specs=pl.BlockSpec((tm, d), lambda i, j: (i, 0)),
        out_shape=jax.ShapeDtypeStruct((m, d), F32),
        scratch_shapes=[pltpu.VMEM((tm, d), BF16)],
        compiler_params=pltpu.CompilerParams(dimension_semantics=("parallel", "arbitrary"),
                                             vmem_limit_bytes=FFN_VMEM_LIMIT),
        name="swiglu_ffn",
    )(x, gains, gains, w_gate_up, w_gate_up, w_down)


def kernel(x, mem, rel_bias, lb_logits, norm_gains, w_in, hg_norm, w_out, w_cq, w_ckv, w_co,
           w_gate_up, w_down):
    batch, seq, d = x.shape
    n_mem = mem.shape[1]
    depth = w_in.shape[0]
    n_norms = norm_gains.shape[1]
    lb_all = _lower_bounds(lb_logits).reshape(depth, 1, HG_WIDTH)
    hg_gain = hg_norm.astype(F32).reshape(depth, 1, HG_WIDTH)
    gains = norm_gains.astype(F32).reshape(depth * n_norms, 1, d)
    rel_bias = rel_bias.astype(F32)
    xf = x.reshape(batch * seq, d).astype(F32)
    memf = mem.reshape(batch * n_mem, d).astype(F32)
    w_in, w_out, w_cq, w_ckv, w_co, w_gate_up, w_down = (
        w.astype(BF16) for w in (w_in, w_out, w_cq, w_ckv, w_co, w_gate_up, w_down))

    for l in range(depth):
        gi = lambda i: l * n_norms + i
        qkv, hproj = _norm_matmul(xf, gains, gi(0), w_in, l, ((3 * ATTN_WIDTH, F32), (4 * HG_WIDTH, BF16)),
                                  tm=1024, tn=1024, name="in_proj")
        attn = _dilated_attention(qkv, rel_bias, batch, seq)
        hg = _hgrn2(hproj, lb_all, hg_gain, l, batch, seq)
        xf = _mix_out(attn, hg, xf, gains, gi(1), w_out, l, tm=512)
        kv, = _norm_matmul(memf, gains, gi(3), w_ckv, l, ((2 * d, BF16),), tm=batch * n_mem, tn=1024,
                           name="mem_kv_proj")
        xf = _cross(xf, gains, gi(2), gi(4), w_cq, kv.reshape(batch, n_mem, 2 * d), w_co, l, seq, tm=512)
        xf = _ffn(xf, gains, gi(5), gi(6), w_gate_up, w_down, l, tm=1024, tf=512)
    return xf.reshape(batch, seq, d).astype(x.dtype)
```

```python
import functools
import math

import numpy as np
import jax
import jax.numpy as jnp
from jax import lax
from jax.experimental import pallas as pl
from jax.experimental.pallas import tpu as pltpu

F32 = jnp.float32
BF16 = jnp.bfloat16

HEAD_DIM = 128
ATTN_HEADS = 8
HG_HEADS = 8
ATTN_WIDTH = ATTN_HEADS * HEAD_DIM
HG_WIDTH = HG_HEADS * HEAD_DIM
Q_BLOCK = 128
DILATED_BRANCHES = ((128, 1), (512, 4), (2048, 16))
ATTN_SPAN = Q_BLOCK * max(d for _, d in DILATED_BRANCHES)
GATHER_STRIDE = 4
REL_BUCKETS = 32
REL_MAX_DIST = 2048
CROSS_HEADS = 4
RMS_EPS = 1e-6
NEG_INF = -1e30
HG_CHUNK = 128
HG_LEVELS = 7
HG_STEP_CHUNKS = 4
LOG2_E = 1.4426950408889634
VMEM_LIMIT = 52 * 1024 * 1024
FFN_VMEM_LIMIT = 58 * 1024 * 1024
FFN_ROW_PARTS = 2


def _rms(x, g):
    return x * lax.rsqrt(jnp.mean(x * x, axis=-1, keepdims=True) + RMS_EPS) * g


def _sigmoid(x):
    return 1.0 / (1.0 + jnp.exp(-x))


def _dot(a, b):
    return jnp.dot(a, b, preferred_element_type=F32)


def _dot_nt(a, b):
    return lax.dot_general(a, b, (((1,), (1,)), ((), ())), preferred_element_type=F32)


def _dot_tn(a, b):
    return lax.dot_general(a, b, (((0,), (0,)), ((), ())), preferred_element_type=F32)


def _params(sem):
    return pltpu.CompilerParams(dimension_semantics=sem, vmem_limit_bytes=VMEM_LIMIT)


def _const_spec(shape):
    nd = len(shape)
    return pl.BlockSpec(shape, lambda *_: (0,) * nd, pipeline_mode=pl.Buffered(1))


def _layer_spec(shape, layer):
    nd = len(shape)
    return pl.BlockSpec((None,) + tuple(shape), lambda *_: (layer,) + (0,) * nd,
                        pipeline_mode=pl.Buffered(1))


def _lb_kernel(z_ref, o_ref):
    z = z_ref[...]
    e = jnp.exp(z - jnp.max(z, axis=0, keepdims=True))
    p = e / jnp.sum(e, axis=0, keepdims=True)
    acc = jnp.zeros_like(p[0:1])
    for l in range(z.shape[0]):
        o_ref[l:l + 1, :] = acc
        acc = acc + p[l:l + 1]


def _lower_bounds(lb_logits):
    return pl.pallas_call(
        _lb_kernel, out_shape=jax.ShapeDtypeStruct(lb_logits.shape, F32),
        name="lower_bounds")(lb_logits.astype(F32))


def _norm_matmul_kernel(x_ref, g_ref, w_ref, *refs, splits):
    o_refs, h_scr = refs[:-1], refs[-1]
    j = pl.program_id(1)
    tm = x_ref.shape[0]

    @pl.when(j == 0)
    def _():
        rows = tm // 2
        for r in range(2):
            sl = slice(r * rows, (r + 1) * rows)
            h = _rms(x_ref[sl, :], g_ref[...]).astype(BF16)
            h_scr[sl, :] = h
            o_refs[0][sl, :] = _dot(h, w_ref[...]).astype(o_refs[0].dtype)

    start = 0
    for o_ref, count in zip(o_refs, splits):
        lo, hi = max(start, 1), start + count
        start = hi
        if hi > lo:
            @pl.when((j >= lo) & (j < hi))
            def _(o_ref=o_ref):
                o_ref[...] = _dot(h_scr[...], w_ref[...]).astype(o_ref.dtype)


def _norm_matmul(x, gains, gi, w, layer, outs, tm, tn, name):
    m, d = x.shape
    n = w.shape[2]
    splits = tuple(cols // tn for cols, _ in outs)
    assert sum(splits) * tn == n

    def out_spec(first, count):
        return pl.BlockSpec((tm, tn), lambda i, j: (i, jnp.clip(j - first, 0, count - 1)))

    firsts = [sum(splits[:k]) for k in range(len(splits))]
    return pl.pallas_call(
        functools.partial(_norm_matmul_kernel, splits=splits),
        grid=(m // tm, n // tn),
        in_specs=[pl.BlockSpec((tm, d), lambda i, j: (i, 0)),
                  _layer_spec((1, d), gi),
                  pl.BlockSpec((None, d, tn), lambda i, j: (layer, 0, j))],
        out_specs=[out_spec(first, count) for first, count in zip(firsts, splits)],
        out_shape=[jax.ShapeDtypeStruct((m, cols), dtype) for cols, dtype in outs],
        scratch_shapes=[pltpu.VMEM((tm, d), BF16)],
        compiler_params=_params(("parallel", "arbitrary")),
        name=name,
    )(x, gains, w)


def _bucket_tables():
    qi = np.arange(Q_BLOCK)[:, None]
    kj = np.arange(2 * Q_BLOCK)[None, :]
    m = qi - kj + Q_BLOCK
    max_exact = REL_BUCKETS // 2
    tables = []
    for window, dilation in DILATED_BRANCHES:
        dist = np.maximum(m, 0) * dilation
        d_f = np.maximum(dist, 1).astype(np.float32)
        large = max_exact + (np.log(d_f / np.float32(max_exact))
                             / np.float32(math.log(REL_MAX_DIST / max_exact))
                             * (REL_BUCKETS - max_exact)).astype(np.int32)
        large = np.minimum(large, REL_BUCKETS - 1)
        bucket = np.where(dist < max_exact, dist, large)
        tables.append(np.where((m >= 0) & (m <= window // dilation), bucket, -1))
    return np.stack(tables).astype(np.int32)


def _dilated_kernel(rb_ref, bucket_ref, q_ref, k_ref, v_ref, o_ref,
                    bias_scr, k_scr, v_scr, acc_scr, m_scr, l_scr,
                    qp_scr, kp_scr, vp_scr, accp_scr, mp_scr, lp_scr):
    h = pl.program_id(1)
    j = pl.program_id(2)
    span = ATTN_SPAN
    group = span // GATHER_STRIDE
    cur_half = (j % 2) * span
    prev_half = span - cur_half

    @pl.when(j == 0)
    def _():
        zeros = jnp.zeros((span, HEAD_DIM), F32)
        k_scr[...] = zeros
        v_scr[...] = zeros
        kp_scr[span:2 * span, :] = zeros
        vp_scr[span:2 * span, :] = zeros
        for g in range(len(DILATED_BRANCHES)):
            bucket = bucket_ref[g]
            acc = jnp.full(bucket.shape, NEG_INF, F32)
            for b in range(REL_BUCKETS):
                acc = jnp.where(bucket == b, rb_ref[b, h], acc)
            bias_scr[g] = acc * LOG2_E

    for c in range(GATHER_STRIDE):
        rows = pl.ds(c, group, stride=GATHER_STRIDE)
        dst = pl.ds(pl.multiple_of(cur_half + c * group, group), group)
        qp_scr[c * group:(c + 1) * group, :] = q_ref[rows, :]
        kp_scr[dst, :] = k_ref[rows, :]
        vp_scr[dst, :] = v_ref[rows, :]

    col = lax.broadcasted_iota(jnp.int32, (Q_BLOCK, 2 * Q_BLOCK), 1)
    before_start = (j == 0) & (col < Q_BLOCK)
    ones = jnp.ones((2 * Q_BLOCK, HEAD_DIM), BF16)
    scale = HEAD_DIM ** -0.5 * LOG2_E

    def strided(start, size, stride):
        return pl.ds(start, size, stride=stride) if stride > 1 else pl.ds(start, size)

    def attend(g, first_block, q, k, v):
        s = _dot_nt((q * scale).astype(BF16), k.astype(BF16)) + bias_scr[g]
        if first_block:
            s = jnp.where(before_start, NEG_INF, s)
        m = jnp.max(s, axis=-1, keepdims=True)
        p = jnp.exp2(s - m).astype(BF16)
        pv = _dot(p, jnp.concatenate([v.astype(BF16), ones], axis=1))
        return pv[:, :HEAD_DIM], pv[:, HEAD_DIM:], jnp.broadcast_to(m, (Q_BLOCK, HEAD_DIM))

    for g, (_, d) in enumerate(DILATED_BRANCHES):
        for i in range(span // Q_BLOCK):
            r, n = i % d, i // d
            if d <= GATHER_STRIDE:
                start = n * Q_BLOCK * d + r
                rows_q = strided(start, Q_BLOCK, d)
                if n == 0:
                    prev = strided(span + start - Q_BLOCK * d, Q_BLOCK, d)
                    k = jnp.concatenate([k_scr[prev, :], k_ref[rows_q, :]], axis=0)
                    v = jnp.concatenate([v_scr[prev, :], v_ref[rows_q, :]], axis=0)
                else:
                    rows_kv = strided(start - Q_BLOCK * d, 2 * Q_BLOCK, d)
                    k, v = k_ref[rows_kv, :], v_ref[rows_kv, :]
                acc, l, m = attend(g, n == 0, q_ref[rows_q, :], k, v)
                acc_scr[g, rows_q, :] = acc
                l_scr[g, rows_q, :] = l
                m_scr[g, rows_q, :] = m
            else:
                inner = d // GATHER_STRIDE
                start = (r % GATHER_STRIDE) * group + n * Q_BLOCK * inner + r // GATHER_STRIDE
                rows_q = strided(start, Q_BLOCK, inner)
                if n == 0:
                    prev = strided(prev_half + start + group - Q_BLOCK * inner, Q_BLOCK, inner)
                else:
                    prev = strided(cur_half + start - Q_BLOCK * inner, Q_BLOCK, inner)
                cur = strided(cur_half + start, Q_BLOCK, inner)
                acc, l, m = attend(g, n == 0, qp_scr[rows_q, :],
                                   jnp.concatenate([kp_scr[prev, :], kp_scr[cur, :]], axis=0),
                                   jnp.concatenate([vp_scr[prev, :], vp_scr[cur, :]], axis=0))
                accp_scr[rows_q, :] = acc
                lp_scr[rows_q, :] = l
                mp_scr[rows_q, :] = m
        if d > GATHER_STRIDE:
            for c in range(GATHER_STRIDE):
                rows = pl.ds(c, group, stride=GATHER_STRIDE)
                acc_scr[g, rows, :] = accp_scr[c * group:(c + 1) * group, :]
                l_scr[g, rows, :] = lp_scr[c * group:(c + 1) * group, :]
                m_scr[g, rows, :] = mp_scr[c * group:(c + 1) * group, :]

    k_scr[...] = k_ref[...]
    v_scr[...] = v_ref[...]

    def merge(t, carry):
        rows = pl.ds(pl.multiple_of(t * Q_BLOCK, Q_BLOCK), Q_BLOCK)
        ms = [m_scr[g, rows, :] for g in range(len(DILATED_BRANCHES))]
        top = jnp.maximum(jnp.maximum(ms[0], ms[1]), ms[2])
        num = jnp.zeros((Q_BLOCK, HEAD_DIM), F32)
        den = jnp.zeros((Q_BLOCK, HEAD_DIM), F32)
        for g in range(len(DILATED_BRANCHES)):
            e = jnp.exp2(ms[g] - top)
            num = num + e * acc_scr[g, rows, :]
            den = den + e * l_scr[g, rows, :]
        o_ref[rows, :] = (num / den).astype(o_ref.dtype)
        return carry

    lax.fori_loop(0, span // Q_BLOCK, merge, 0)


def _dilated_attention(qkv, rel_bias, batch, seq):
    span = ATTN_SPAN
    nspan = seq // span
    nbr = len(DILATED_BRANCHES)
    buckets = jnp.asarray(_bucket_tables())

    def part(p):
        return pl.BlockSpec((span, HEAD_DIM), lambda b, h, j: (b * nspan + j, p * ATTN_HEADS + h))

    stat = pltpu.VMEM((nbr, span, HEAD_DIM), F32)
    one_span = pltpu.VMEM((span, HEAD_DIM), F32)
    two_spans = pltpu.VMEM((2 * span, HEAD_DIM), F32)
    return pl.pallas_call(
        _dilated_kernel,
        grid=(batch, ATTN_HEADS, nspan),
        in_specs=[pl.BlockSpec(memory_space=pltpu.SMEM),
                  pl.BlockSpec(buckets.shape, lambda b, h, j: (0, 0, 0)),
                  part(0), part(1), part(2)],
        out_specs=pl.BlockSpec((span, HEAD_DIM), lambda b, h, j: (b * nspan + j, h)),
        out_shape=jax.ShapeDtypeStruct((batch * seq, ATTN_WIDTH), BF16),
        scratch_shapes=[pltpu.VMEM((nbr, Q_BLOCK, 2 * Q_BLOCK), F32),
                        one_span, one_span, stat, stat, stat,
                        one_span, two_spans, two_spans, one_span, one_span, one_span],
        compiler_params=_params(("arbitrary", "arbitrary", "arbitrary")),
        name="dilated_attention",
    )(rel_bias, buckets, qkv, qkv, qkv)


def _hg_tables():
    c = HG_CHUNK
    t = np.arange(c)[:, None]
    u = np.arange(c)[None, :]
    mats = []
    for j in range(1, HG_LEVELS):
        s = 1 << j
        upper = (t & s) != 0
        q_side = (u >= (t & ~(s - 1))) & (u <= t)
        k_side = (u > t) & (u <= (t | (s - 1)))
        mats.append(np.where(upper, q_side, k_side))
    mats.append(u <= t)
    mats.append(u > t)
    stack = np.concatenate(mats, axis=0).astype(np.float32)
    level = np.where(u < t, np.floor(np.log2(np.maximum(t ^ u, 1))).astype(np.int32), -1)
    level = np.where(u == t, HG_LEVELS, level).astype(np.int32)
    return stack, level


def _hgrn_kernel(stack_ref, level_ref, lb_ref, gain_ref, fz_ref, iv_ref, qz_ref, gz_ref, o_ref, st_scr):
    @pl.when(pl.program_id(1) == 0)
    def _():
        st_scr[...] = jnp.zeros_like(st_scr)

    c = HG_CHUNK
    pairs = [slice(2 * p * HEAD_DIM, 2 * (p + 1) * HEAD_DIM) for p in range(HG_HEADS // 2)]
    halves = (slice(0, HEAD_DIM), slice(HEAD_DIM, 2 * HEAD_DIM))
    units = [(slice(ci * c, (ci + 1) * c), sl) for ci in range(HG_STEP_CHUNKS) for sl in pairs]
    level = level_ref[...]

    f, kk, qq, ex = [], [], [], []
    for rows, sl in units:
        lb = lb_ref[:, sl]
        f_u = lb + (1.0 - lb) * _sigmoid(fz_ref[rows, sl].astype(F32))
        lf = jnp.log(f_u) * LOG2_E
        hi = lf.astype(BF16)
        lo = (lf - hi.astype(F32)).astype(BF16)
        ex.append(_dot(stack_ref[...], jnp.concatenate([hi, lo], axis=0)))
        qz = qz_ref[rows, sl].astype(F32)
        f.append(f_u)
        kk.append(1.0 - f_u)
        qq.append(qz * _sigmoid(qz))

    odd = (lax.broadcasted_iota(jnp.int32, (c, 2 * HEAD_DIM), 0) & 1) != 0
    a = [[jnp.zeros((c, c), F32) for _ in halves] for _ in units]
    for j in range(HG_LEVELS + 1):
        for u in range(len(units)):
            if j == 0:
                qy, ky = (qq[u] * jnp.where(odd, f[u], 1.0)).astype(BF16), kk[u].astype(BF16)
            elif j < HG_LEVELS:
                y = jnp.exp2(ex[u][(j - 1) * c:j * c])
                qy, ky = (qq[u] * y).astype(BF16), (kk[u] * y).astype(BF16)
            else:
                qy, ky = qq[u].astype(BF16), kk[u].astype(BF16)
            for i, half in enumerate(halves):
                a[u][i] = jnp.where(level == j, _dot_nt(qy[:, half], ky[:, half]), a[u][i])

    for u, (rows, sl) in enumerate(units):
        p = u % len(pairs)
        eb = jnp.exp2(ex[u][(HG_LEVELS - 1) * c:HG_LEVELS * c])
        er = jnp.exp2(ex[u][HG_LEVELS * c:(HG_LEVELS + 1) * c])
        qe = (qq[u] * eb).astype(BF16)
        ke = (kk[u] * er).astype(BF16)
        vv = iv_ref[rows, sl].astype(BF16)
        outs = []
        for i, half in enumerate(halves):
            h = 2 * p + i
            st = st_scr[h]
            o = _dot_nt(qe[:, half], st.astype(BF16)) + _dot(a[u][i].astype(BF16), vv[:, half])
            st_scr[h] = st * eb[c - 1:c, half] + _dot_tn(vv[:, half], ke[:, half])
            outs.append(o * lax.rsqrt(jnp.mean(o * o, axis=-1, keepdims=True) + RMS_EPS))
        gz = gz_ref[rows, sl].astype(F32)
        o = jnp.concatenate(outs, axis=1) * gain_ref[:, sl]
        o_ref[rows, sl] = (o * (gz * _sigmoid(gz))).astype(o_ref.dtype)


def _hgrn2(proj, lb, gain, layer, batch, seq):
    stack, level = _hg_tables()
    stack2 = np.concatenate([stack, stack], axis=1)
    c = HG_CHUNK * HG_STEP_CHUNKS
    nchunk = seq // c

    def part(g):
        return pl.BlockSpec((c, HG_WIDTH), lambda b, n: (b * nchunk + n, g))

    return pl.pallas_call(
        _hgrn_kernel,
        grid=(batch, nchunk),
        in_specs=[_const_spec(stack2.shape), _const_spec(level.shape),
                  _layer_spec((1, HG_WIDTH), layer), _layer_spec((1, HG_WIDTH), layer),
                  part(0), part(1), part(2), part(3)],
        out_specs=pl.BlockSpec((c, HG_WIDTH), lambda b, n: (b * nchunk + n, 0)),
        out_shape=jax.ShapeDtypeStruct((batch * seq, HG_WIDTH), BF16),
        scratch_shapes=[pltpu.VMEM((HG_HEADS, HEAD_DIM, HEAD_DIM), F32)],
        compiler_params=_params(("parallel", "arbitrary")),
        name="hgrn2",
    )(jnp.asarray(stack2, BF16), jnp.asarray(level), lb, gain, proj, proj, proj, proj)


def _mix_out_kernel(attn_ref, hg_ref, x_ref, g_ref, w_ref, out_ref):
    y = _dot(attn_ref[...], w_ref[:ATTN_WIDTH, :]) + _dot(hg_ref[...], w_ref[ATTN_WIDTH:, :])
    out_ref[...] = x_ref[...] + _rms(y, g_ref[...])


def _mix_out(attn, hg, x, gains, gi, w, layer, tm):
    m, d = x.shape
    row = lambda width: pl.BlockSpec((tm, width), lambda i: (i, 0))
    return pl.pallas_call(
        _mix_out_kernel,
        grid=(m // tm,),
        in_specs=[row(ATTN_WIDTH), row(HG_WIDTH), row(d), _layer_spec((1, d), gi),
                  _layer_spec(w.shape[1:], layer)],
        out_specs=row(d),
        out_shape=jax.ShapeDtypeStruct((m, d), F32),
        compiler_params=_params(("parallel",)),
        name="mix_out",
    )(attn, hg, x, gains, w)


def _cross_kernel(x_ref, gq_ref, go_ref, wq_ref, kv_ref, wo_ref, out_ref, co_scr):
    x = x_ref[...]
    d = x.shape[-1]
    hd = d // CROSS_HEADS
    hc = _rms(x, gq_ref[...]).astype(BF16)
    cq = _dot(hc, wq_ref[...]).astype(BF16)
    scale = hd ** -0.5 * LOG2_E
    for h in range(CROSS_HEADS):
        sl = slice(h * hd, (h + 1) * hd)
        s = _dot_nt(cq[:, sl], kv_ref[:, sl]) * scale
        p = jnp.exp2(s - jnp.max(s, axis=-1, keepdims=True))
        p = p * (1.0 / jnp.sum(p, axis=-1, keepdims=True))
        co_scr[:, sl] = _dot(p.astype(BF16), kv_ref[:, d + h * hd:d + (h + 1) * hd]).astype(BF16)
    y = _dot(co_scr[...], wo_ref[...])
    out_ref[...] = x + _rms(y, go_ref[...])


def _cross(x, gains, gq, go, wq, kv, wo, layer, seq, tm):
    m, d = x.shape
    n_mem = kv.shape[1]
    per_batch = seq // tm
    return pl.pallas_call(
        _cross_kernel,
        grid=(m // tm,),
        in_specs=[pl.BlockSpec((tm, d), lambda i: (i, 0)),
                  _layer_spec((1, d), gq), _layer_spec((1, d), go), _layer_spec(wq.shape[1:], layer),
                  pl.BlockSpec((None, n_mem, 2 * d), lambda i: (i // per_batch, 0, 0)),
                  _layer_spec(wo.shape[1:], layer)],
        out_specs=pl.BlockSpec((tm, d), lambda i: (i, 0)),
        out_shape=jax.ShapeDtypeStruct((m, d), F32),
        scratch_shapes=[pltpu.VMEM((tm, d), BF16)],
        compiler_params=_params(("parallel",)),
        name="cross_attention",
    )(x, gains, gains, wq, kv, wo)


def _ffn_kernel(x_ref, gi_ref, go_ref, wg_ref, wu_ref, wd_ref, out_ref, h_scr):
    j = pl.program_id(1)
    last = pl.num_programs(1) - 1
    rows = x_ref.shape[0] // FFN_ROW_PARTS

    def step(first, final):
        for r in range(FFN_ROW_PARTS):
            sl = slice(r * rows, (r + 1) * rows)
            if first:
                h = _rms(x_ref[sl, :], gi_ref[...]).astype(BF16)
                h_scr[sl, :] = h
            else:
                h = h_scr[sl, :]
            gate = _dot(h, wg_ref[...])
            up = _dot(h, wu_ref[...])
            act = (gate * _sigmoid(gate) * up).astype(BF16)
            acc = _dot(act, wd_ref[...])
            if not first:
                acc = out_ref[sl, :] + acc
            if final:
                acc = x_ref[sl, :] + _rms(acc, go_ref[...])
            out_ref[sl, :] = acc

    pl.when(j == 0)(lambda: step(True, False))
    pl.when((j > 0) & (j < last))(lambda: step(False, False))
    pl.when(j == last)(lambda: step(False, True))


def _ffn(x, gains, gi, go, w_gate_up, w_down, layer, tm, tf):
    m, d = x.shape
    nf = w_down.shape[1] // tf
    return pl.pallas_call(
        _ffn_kernel,
        grid=(m // tm, nf),
        in_specs=[pl.BlockSpec((tm, d), lambda i, j: (i, 0)),
                  _layer_spec((1, d), gi), _layer_spec((1, d), go),
                  pl.BlockSpec((None, d, tf), lambda i, j: (layer, 0, j)),
                  pl.BlockSpec((None, d, tf), lambda i, j: (layer, 0, nf + j)),
                  pl.BlockSpec((None, tf, d), lambda i, j: (layer, j, 0))],
        out_specs=pl.BlockSpec((tm, d), lambda i, j: (i, 0)),
        out_shape=jax.ShapeDtypeStruct((m, d), F32),
        scratch_shapes=[pltpu.VMEM((tm, d), BF16)],
        compiler_params=pltpu.CompilerParams(dimension_semantics=("parallel", "arbitrary"),
                                             vmem_limit_bytes=FFN_VMEM_LIMIT),
        name="swiglu_ffn",
    )(x, gains, gains, w_gate_up, w_gate_up, w_down)


def kernel(x, mem, rel_bias, lb_logits, norm_gains, w_in, hg_norm, w_out, w_cq, w_ckv, w_co,
           w_gate_up, w_down):
    batch, seq, d = x.shape
    n_mem = mem.shape[1]
    depth = w_in.shape[0]
    n_norms = norm_gains.shape[1]
    lb_all = _lower_bounds(lb_logits).reshape(depth, 1, HG_WIDTH)
    hg_gain = hg_norm.astype(F32).reshape(depth, 1, HG_WIDTH)
    gains = norm_gains.astype(F32).reshape(depth * n_norms, 1, d)
    rel_bias = rel_bias.astype(F32)
    xf = x.reshape(batch * seq, d).astype(F32)
    memf = mem.reshape(batch * n_mem, d).astype(F32)
    w_in, w_out, w_cq, w_ckv, w_co, w_gate_up, w_down = (
        w.astype(BF16) for w in (w_in, w_out, w_cq, w_ckv, w_co, w_gate_up, w_down))

    for l in range(depth):
        gi = lambda i: l * n_norms + i
        qkv, hproj = _norm_matmul(xf, gains, gi(0), w_in, l, ((3 * ATTN_WIDTH, F32), (4 * HG_WIDTH, BF16)),
                                  tm=1024, tn=1024, name="in_proj")
        attn = _dilated_attention(qkv, rel_bias, batch, seq)
        hg = _hgrn2(hproj, lb_all, hg_gain, l, batch, seq)
        xf = _mix_out(attn, hg, xf, gains, gi(1), w_out, l, tm=512)
        kv, = _norm_matmul(memf, gains, gi(3), w_ckv, l, ((2 * d, BF16),), tm=batch * n_mem, tn=1024,
                           name="mem_kv_proj")
        xf = _cross(xf, gains, gi(2), gi(4), w_cq, kv.reshape(batch, n_mem, 2 * d), w_co, l, seq, tm=512)
        xf = _ffn(xf, gains, gi(5), gi(6), w_gate_up, w_down, l, tm=1024, tf=512)
    return xf.reshape(batch, seq, d).astype(x.dtype)
```

```python
import functools
import math

import numpy as np
import jax
import jax.numpy as jnp
from jax import lax
from jax.experimental import pallas as pl
from jax.experimental.pallas import tpu as pltpu

F32 = jnp.float32
BF16 = jnp.bfloat16

HEAD_DIM = 128
ATTN_HEADS = 8
HG_HEADS = 8
ATTN_WIDTH = ATTN_HEADS * HEAD_DIM
HG_WIDTH = HG_HEADS * HEAD_DIM
Q_BLOCK = 128
DILATED_BRANCHES = ((128, 1), (512, 4), (2048, 16))
ATTN_SPAN = Q_BLOCK * max(d for _, d in DILATED_BRANCHES)
GATHER_STRIDE = 4
REL_BUCKETS = 32
REL_MAX_DIST = 2048
CROSS_HEADS = 4
RMS_EPS = 1e-6
NEG_INF = -1e30
HG_CHUNK = 128
HG_LEVELS = 7
HG_STEP_CHUNKS = 4
LOG2_E = 1.4426950408889634
VMEM_LIMIT = 52 * 1024 * 1024
FFN_VMEM_LIMIT = 58 * 1024 * 1024
FFN_ROW_PARTS = 2


def _rms(x, g):
    return x * lax.rsqrt(jnp.mean(x * x, axis=-1, keepdims=True) + RMS_EPS) * g


def _sigmoid(x):
    return 1.0 / (1.0 + jnp.exp(-x))


def _dot(a, b):
    return jnp.dot(a, b, preferred_element_type=F32)


def _dot_nt(a, b):
    return lax.dot_general(a, b, (((1,), (1,)), ((), ())), preferred_element_type=F32)


def _dot_tn(a, b):
    return lax.dot_general(a, b, (((0,), (0,)), ((), ())), preferred_element_type=F32)


def _params(sem):
    return pltpu.CompilerParams(dimension_semantics=sem, vmem_limit_bytes=VMEM_LIMIT)


def _const_spec(shape):
    nd = len(shape)
    return pl.BlockSpec(shape, lambda *_: (0,) * nd, pipeline_mode=pl.Buffered(1))


def _layer_spec(shape, layer):
    nd = len(shape)
    return pl.BlockSpec((None,) + tuple(shape), lambda *_: (layer,) + (0,) * nd,
                        pipeline_mode=pl.Buffered(1))


def _lb_kernel(z_ref, o_ref):
    z = z_ref[...]
    e = jnp.exp(z - jnp.max(z, axis=0, keepdims=True))
    p = e / jnp.sum(e, axis=0, keepdims=True)
    acc = jnp.zeros_like(p[0:1])
    for l in range(z.shape[0]):
        o_ref[l:l + 1, :] = acc
        acc = acc + p[l:l + 1]


def _lower_bounds(lb_logits):
    return pl.pallas_call(
        _lb_kernel, out_shape=jax.ShapeDtypeStruct(lb_logits.shape, F32),
        name="lower_bounds")(lb_logits.astype(F32))


def _norm_matmul_kernel(x_ref, g_ref, w_ref, *refs, splits):
    o_refs, h_scr = refs[:-1], refs[-1]
    j = pl.program_id(1)
    tm = x_ref.shape[0]

    @pl.when(j == 0)
    def _():
        rows = tm // 2
        for r in range(2):
            sl = slice(r * rows, (r + 1) * rows)
            h = _rms(x_ref[sl, :], g_ref[...]).astype(BF16)
            h_scr[sl, :] = h
            o_refs[0][sl, :] = _dot(h, w_ref[...]).astype(o_refs[0].dtype)

    start = 0
    for o_ref, count in zip(o_refs, splits):
        lo, hi = max(start, 1), start + count
        start = hi
        if hi > lo:
            @pl.when((j >= lo) & (j < hi))
            def _(o_ref=o_ref):
                o_ref[...] = _dot(h_scr[...], w_ref[...]).astype(o_ref.dtype)


def _norm_matmul(x, gains, gi, w, layer, outs, tm, tn, name):
    m, d = x.shape
    n = w.shape[2]
    splits = tuple(cols // tn for cols, _ in outs)
    assert sum(splits) * tn == n

    def out_spec(first, count):
        return pl.BlockSpec((tm, tn), lambda i, j: (i, jnp.clip(j - first, 0, count - 1)))

    firsts = [sum(splits[:k]) for k in range(len(splits))]
    return pl.pallas_call(
        functools.partial(_norm_matmul_kernel, splits=splits),
        grid=(m // tm, n // tn),
        in_specs=[pl.BlockSpec((tm, d), lambda i, j: (i, 0)),
                  _layer_spec((1, d), gi),
                  pl.BlockSpec((None, d, tn), lambda i, j: (layer, 0, j))],
        out_specs=[out_spec(first, count) for first, count in zip(firsts, splits)],
        out_shape=[jax.ShapeDtypeStruct((m, cols), dtype) for cols, dtype in outs],
        scratch_shapes=[pltpu.VMEM((tm, d), BF16)],
        compiler_params=_params(("parallel", "arbitrary")),
        name=name,
    )(x, gains, w)


def _bucket_tables():
    qi = np.arange(Q_BLOCK)[:, None]
    kj = np.arange(2 * Q_BLOCK)[None, :]
    m = qi - kj + Q_BLOCK
    max_exact = REL_BUCKETS // 2
    tables = []
    for window, dilation in DILATED_BRANCHES:
        dist = np.maximum(m, 0) * dilation
        d_f = np.maximum(dist, 1).astype(np.float32)
        large = max_exact + (np.log(d_f / np.float32(max_exact))
                             / np.float32(math.log(REL_MAX_DIST / max_exact))
                             * (REL_BUCKETS - max_exact)).astype(np.int32)
        large = np.minimum(large, REL_BUCKETS - 1)
        bucket = np.where(dist < max_exact, dist, large)
        tables.append(np.where((m >= 0) & (m <= window // dilation), bucket, -1))
    return np.stack(tables).astype(np.int32)


def _dilated_kernel(rb_ref, bucket_ref, q_ref, k_ref, v_ref, o_ref,
                    bias_scr, k_scr, v_scr, acc_scr, m_scr, l_scr,
                    qp_scr, kp_scr, vp_scr, accp_scr, mp_scr, lp_scr):
    h = pl.program_id(1)
    j = pl.program_id(2)
    span = ATTN_SPAN
    group = span // GATHER_STRIDE
    cur_half = (j % 2) * span
    prev_half = span - cur_half

    @pl.when(j == 0)
    def _():
        zeros = jnp.zeros((span, HEAD_DIM), F32)
        k_scr[...] = zeros
        v_scr[...] = zeros
        kp_scr[span:2 * span, :] = zeros
        vp_scr[span:2 * span, :] = zeros
        for g in range(len(DILATED_BRANCHES)):
            bucket = bucket_ref[g]
            acc = jnp.full(bucket.shape, NEG_INF, F32)
            for b in range(REL_BUCKETS):
                acc = jnp.where(bucket == b, rb_ref[b, h], acc)
            bias_scr[g] = acc * LOG2_E

    for c in range(GATHER_STRIDE):
        rows = pl.ds(c, group, stride=GATHER_STRIDE)
        dst = pl.ds(pl.multiple_of(cur_half + c * group, group), group)
        qp_scr[c * group:(c + 1) * group, :] = q_ref[rows, :]
        kp_scr[dst, :] = k_ref[rows, :]
        vp_scr[dst, :] = v_ref[rows, :]

    col = lax.broadcasted_iota(jnp.int32, (Q_BLOCK, 2 * Q_BLOCK), 1)
    before_start = (j == 0) & (col < Q_BLOCK)
    ones = jnp.ones((2 * Q_BLOCK, HEAD_DIM), BF16)
    scale = HEAD_DIM ** -0.5 * LOG2_E

    def strided(start, size, stride):
        return pl.ds(start, size, stride=stride) if stride > 1 else pl.ds(start, size)

    def attend(g, first_block, q, k, v):
        s = _dot_nt((q * scale).astype(BF16), k.astype(BF16)) + bias_scr[g]
        if first_block:
            s = jnp.where(before_start, NEG_INF, s)
        m = jnp.max(s, axis=-1, keepdims=True)
        p = jnp.exp2(s - m).astype(BF16)
        pv = _dot(p, jnp.concatenate([v.astype(BF16), ones], axis=1))
        return pv[:, :HEAD_DIM], pv[:, HEAD_DIM:], jnp.broadcast_to(m, (Q_BLOCK, HEAD_DIM))

    for g, (_, d) in enumerate(DILATED_BRANCHES):
        for i in range(span // Q_BLOCK):
            r, n = i % d, i // d
            if d <= GATHER_STRIDE:
                start = n * Q_BLOCK * d + r
                rows_q = strided(start, Q_BLOCK, d)
                if n == 0:
                    prev = strided(span + start - Q_BLOCK * d, Q_BLOCK, d)
                    k = jnp.concatenate([k_scr[prev, :], k_ref[rows_q, :]], axis=0)
                    v = jnp.concatenate([v_scr[prev, :], v_ref[rows_q, :]], axis=0)
                else:
                    rows_kv = strided(start - Q_BLOCK * d, 2 * Q_BLOCK, d)
                    k, v = k_ref[rows_kv, :], v_ref[rows_kv, :]
                acc, l, m = attend(g, n == 0, q_ref[rows_q, :], k, v)
                acc_scr[g, rows_q, :] = acc
                l_scr[g, rows_q, :] = l
                m_scr[g, rows_q, :] = m
            else:
                inner = d // GATHER_STRIDE
                start = (r % GATHER_STRIDE) * group + n * Q_BLOCK * inner + r // GATHER_STRIDE
                rows_q = strided(start, Q_BLOCK, inner)
                if n == 0:
                    prev = strided(prev_half + start + group - Q_BLOCK * inner, Q_BLOCK, inner)
                else:
                    prev = strided(cur_half + start - Q_BLOCK * inner, Q_BLOCK, inner)
                cur = strided(cur_half + start, Q_BLOCK, inner)
                acc, l, m = attend(g, n == 0, qp_scr[rows_q, :],
                                   jnp.concatenate([kp_scr[prev, :], kp_scr[cur, :]], axis=0),
                                   jnp.concatenate([vp_scr[prev, :], vp_scr[cur, :]], axis=0))
                accp_scr[rows_q, :] = acc
                lp_scr[rows_q, :] = l
                mp_scr[rows_q, :] = m
        if d > GATHER_STRIDE:
            for c in range(GATHER_STRIDE):
                rows = pl.ds(c, group, stride=GATHER_STRIDE)
                acc_scr[g, rows, :] = accp_scr[c * group:(c + 1) * group, :]
                l_scr[g, rows, :] = lp_scr[c * group:(c + 1) * group, :]
                m_scr[g, rows, :] = mp_scr[c * group:(c + 1) * group, :]

    k_scr[...] = k_ref[...]
    v_scr[...] = v_ref[...]

    def merge(t, carry):
        rows = pl.ds(pl.multiple_of(t * Q_BLOCK, Q_BLOCK), Q_BLOCK)
        ms = [m_scr[g, rows, :] for g in range(len(DILATED_BRANCHES))]
        top = jnp.maximum(jnp.maximum(ms[0], ms[1]), ms[2])
        num = jnp.zeros((Q_BLOCK, HEAD_DIM), F32)
        den = jnp.zeros((Q_BLOCK, HEAD_DIM), F32)
        for g in range(len(DILATED_BRANCHES)):
            e = jnp.exp2(ms[g] - top)
            num = num + e * acc_scr[g, rows, :]
            den = den + e * l_scr[g, rows, :]
        o_ref[rows, :] = (num / den).astype(o_ref.dtype)
        return carry

    lax.fori_loop(0, span // Q_BLOCK, merge, 0)


def _dilated_attention(qkv, rel_bias, batch, seq):
    span = ATTN_SPAN
    nspan = seq // span
    nbr = len(DILATED_BRANCHES)
    buckets = jnp.asarray(_bucket_tables())

    def part(p):
        return pl.BlockSpec((span, HEAD_DIM), lambda b, h, j: (b * nspan + j, p * ATTN_HEADS + h))

    stat = pltpu.VMEM((nbr, span, HEAD_DIM), F32)
    one_span = pltpu.VMEM((span, HEAD_DIM), F32)
    two_spans = pltpu.VMEM((2 * span, HEAD_DIM), F32)
    return pl.pallas_call(
        _dilated_kernel,
        grid=(batch, ATTN_HEADS, nspan),
        in_specs=[pl.BlockSpec(memory_space=pltpu.SMEM),
                  pl.BlockSpec(buckets.shape, lambda b, h, j: (0, 0, 0)),
                  part(0), part(1), part(2)],
        out_specs=pl.BlockSpec((span, HEAD_DIM), lambda b, h, j: (b * nspan + j, h)),
        out_shape=jax.ShapeDtypeStruct((batch * seq, ATTN_WIDTH), BF16),
        scratch_shapes=[pltpu.VMEM((nbr, Q_BLOCK, 2 * Q_BLOCK), F32),
                        one_span, one_span, stat, stat, stat,
                        one_span, two_spans, two_spans, one_span, one_span, one_span],
        compiler_params=_params(("arbitrary", "arbitrary", "arbitrary")),
        name="dilated_attention",
    )(rel_bias, buckets, qkv, qkv, qkv)


def _hg_tables():
    c = HG_CHUNK
    t = np.arange(c)[:, None]
    u = np.arange(c)[None, :]
    mats = []
    for j in range(1, HG_LEVELS):
        s = 1 << j
        upper = (t & s) != 0
        q_side = (u >= (t & ~(s - 1))) & (u <= t)
        k_side = (u > t) & (u <= (t | (s - 1)))
        mats.append(np.where(upper, q_side, k_side))
    mats.append(u <= t)
    mats.append(u > t)
    stack = np.concatenate(mats, axis=0).astype(np.float32)
    level = np.where(u < t, np.floor(np.log2(np.maximum(t ^ u, 1))).astype(np.int32), -1)
    level = np.where(u == t, HG_LEVELS, level).astype(np.int32)
    return stack, level


def _hgrn_kernel(stack_ref, level_ref, lb_ref, gain_ref, fz_ref, iv_ref, qz_ref, gz_ref, o_ref, st_scr):
    @pl.when(pl.program_id(1) == 0)
    def _():
        st_scr[...] = jnp.zeros_like(st_scr)

    c = HG_CHUNK
    pairs = [slice(2 * p * HEAD_DIM, 2 * (p + 1) * HEAD_DIM) for p in range(HG_HEADS // 2)]
    halves = (slice(0, HEAD_DIM), slice(HEAD_DIM, 2 * HEAD_DIM))
    units = [(slice(ci * c, (ci + 1) * c), sl) for ci in range(HG_STEP_CHUNKS) for sl in pairs]
    level = level_ref[...]

    f, kk, qq, ex = [], [], [], []
    for rows, sl in units:
        lb = lb_ref[:, sl]
        f_u = lb + (1.0 - lb) * _sigmoid(fz_ref[rows, sl].astype(F32))
        lf = jnp.log(f_u) * LOG2_E
        hi = lf.astype(BF16)
        lo = (lf - hi.astype(F32)).astype(BF16)
        ex.append(_dot(stack_ref[...], jnp.concatenate([hi, lo], axis=0)))
        qz = qz_ref[rows, sl].astype(F32)
        f.append(f_u)
        kk.append((1.0 - f_u).astype(BF16))
        qq.append((qz * _sigmoid(qz)).astype(BF16))

    odd = (lax.broadcasted_iota(jnp.int32, (c, 2 * HEAD_DIM), 0) & 1) != 0
    a = [[jnp.zeros((c, c), F32) for _ in halves] for _ in units]
    for j in range(HG_LEVELS + 1):
        for u in range(len(units)):
            if j == 0:
                qy, ky = qq[u] * jnp.where(odd, f[u], 1.0).astype(BF16), kk[u]
            elif j < HG_LEVELS:
                y = jnp.exp2(ex[u][(j - 1) * c:j * c]).astype(BF16)
                qy, ky = qq[u] * y, kk[u] * y
            else:
                qy, ky = qq[u], kk[u]
            for i, half in enumerate(halves):
                a[u][i] = jnp.where(level == j, _dot_nt(qy[:, half], ky[:, half]), a[u][i])

    for u, (rows, sl) in enumerate(units):
        p = u % len(pairs)
        eb = jnp.exp2(ex[u][(HG_LEVELS - 1) * c:HG_LEVELS * c])
        er = jnp.exp2(ex[u][HG_LEVELS * c:(HG_LEVELS + 1) * c])
        qe = qq[u] * eb.astype(BF16)
        ke = kk[u] * er.astype(BF16)
        vv = iv_ref[rows, sl].astype(BF16)
        outs = []
        for i, half in enumerate(halves):
            h = 2 * p + i
            st = st_scr[h]
            o = _dot_nt(qe[:, half], st.astype(BF16)) + _dot(a[u][i].astype(BF16), vv[:, half])
            st_scr[h] = st * eb[c - 1:c, half] + _dot_tn(vv[:, half], ke[:, half])
            outs.append(o * lax.rsqrt(jnp.mean(o * o, axis=-1, keepdims=True) + RMS_EPS))
        gz = gz_ref[rows, sl].astype(F32)
        o = jnp.concatenate(outs, axis=1) * gain_ref[:, sl]
        o_ref[rows, sl] = (o * (gz * _sigmoid(gz))).astype(o_ref.dtype)


def _hgrn2(proj, lb, gain, layer, batch, seq):
    stack, level = _hg_tables()
    stack2 = np.concatenate([stack, stack], axis=1)
    c = HG_CHUNK * HG_STEP_CHUNKS
    nchunk = seq // c

    def part(g):
        return pl.BlockSpec((c, HG_WIDTH), lambda b, n: (b * nchunk + n, g))

    return pl.pallas_call(
        _hgrn_kernel,
        grid=(batch, nchunk),
        in_specs=[_const_spec(stack2.shape), _const_spec(level.shape),
                  _layer_spec((1, HG_WIDTH), layer), _layer_spec((1, HG_WIDTH), layer),
                  part(0), part(1), part(2), part(3)],
        out_specs=pl.BlockSpec((c, HG_WIDTH), lambda b, n: (b * nchunk + n, 0)),
        out_shape=jax.ShapeDtypeStruct((batch * seq, HG_WIDTH), BF16),
        scratch_shapes=[pltpu.VMEM((HG_HEADS, HEAD_DIM, HEAD_DIM), F32)],
        compiler_params=_params(("parallel", "arbitrary")),
        name="hgrn2",
    )(jnp.asarray(stack2, BF16), jnp.asarray(level), lb, gain, proj, proj, proj, proj)


def _mix_out_kernel(attn_ref, hg_ref, x_ref, g_ref, w_ref, out_ref):
    y = _dot(attn_ref[...], w_ref[:ATTN_WIDTH, :]) + _dot(hg_ref[...], w_ref[ATTN_WIDTH:, :])
    out_ref[...] = x_ref[...] + _rms(y, g_ref[...])


def _mix_out(attn, hg, x, gains, gi, w, layer, tm):
    m, d = x.shape
    row = lambda width: pl.BlockSpec((tm, width), lambda i: (i, 0))
    return pl.pallas_call(
        _mix_out_kernel,
        grid=(m // tm,),
        in_specs=[row(ATTN_WIDTH), row(HG_WIDTH), row(d), _layer_spec((1, d), gi),
                  _layer_spec(w.shape[1:], layer)],
        out_specs=row(d),
        out_shape=jax.ShapeDtypeStruct((m, d), F32),
        compiler_params=_params(("parallel",)),
        name="mix_out",
    )(attn, hg, x, gains, w)


def _cross_kernel(x_ref, gq_ref, go_ref, wq_ref, kv_ref, wo_ref, out_ref, co_scr):
    x = x_ref[...]
    d = x.shape[-1]
    hd = d // CROSS_HEADS
    hc = _rms(x, gq_ref[...]).astype(BF16)
    cq = _dot(hc, wq_ref[...]).astype(BF16)
    scale = hd ** -0.5 * LOG2_E
    for h in range(CROSS_HEADS):
        sl = slice(h * hd, (h + 1) * hd)
        s = _dot_nt(cq[:, sl], kv_ref[:, sl]) * scale
        p = jnp.exp2(s - jnp.max(s, axis=-1, keepdims=True))
        p = p * (1.0 / jnp.sum(p, axis=-1, keepdims=True))
        co_scr[:, sl] = _dot(p.astype(BF16), kv_ref[:, d + h * hd:d + (h + 1) * hd]).astype(BF16)
    y = _dot(co_scr[...], wo_ref[...])
    out_ref[...] = x + _rms(y, go_ref[...])


def _cross(x, gains, gq, go, wq, kv, wo, layer, seq, tm):
    m, d = x.shape
    n_mem = kv.shape[1]
    per_batch = seq // tm
    return pl.pallas_call(
        _cross_kernel,
        grid=(m // tm,),
        in_specs=[pl.BlockSpec((tm, d), lambda i: (i, 0)),
                  _layer_spec((1, d), gq), _layer_spec((1, d), go), _layer_spec(wq.shape[1:], layer),
                  pl.BlockSpec((None, n_mem, 2 * d), lambda i: (i // per_batch, 0, 0)),
                  _layer_spec(wo.shape[1:], layer)],
        out_specs=pl.BlockSpec((tm, d), lambda i: (i, 0)),
        out_shape=jax.ShapeDtypeStruct((m, d), F32),
        scratch_shapes=[pltpu.VMEM((tm, d), BF16)],
        compiler_params=_params(("parallel",)),
        name="cross_attention",
    )(x, gains, gains, wq, kv, wo)


def _ffn_kernel(x_ref, gi_ref, go_ref, wg_ref, wu_ref, wd_ref, out_ref, h_scr):
    j = pl.program_id(1)
    last = pl.num_programs(1) - 1
    rows = x_ref.shape[0] // FFN_ROW_PARTS

    def step(first, final):
        for r in range(FFN_ROW_PARTS):
            sl = slice(r * rows, (r + 1) * rows)
            if first:
                h = _rms(x_ref[sl, :], gi_ref[...]).astype(BF16)
                h_scr[sl, :] = h
            else:
                h = h_scr[sl, :]
            gate = _dot(h, wg_ref[...])
            up = _dot(h, wu_ref[...])
            act = (gate * _sigmoid(gate) * up).astype(BF16)
            acc = _dot(act, wd_ref[...])
            if not first:
                acc = out_ref[sl, :] + acc
            if final:
                acc = x_ref[sl, :] + _rms(acc, go_ref[...])
            out_ref[sl, :] = acc

    pl.when(j == 0)(lambda: step(True, False))
    pl.when((j > 0) & (j < last))(lambda: step(False, False))
    pl.when(j == last)(lambda: step(False, True))


def _ffn(x, gains, gi, go, w_gate_up, w_down, layer, tm, tf):
    m, d = x.shape
    nf = w_down.shape[1] // tf
    return pl.pallas_call(
        _ffn_kernel,
        grid=(m // tm, nf),
        in_specs=[pl.BlockSpec((tm, d), lambda i, j: (i, 0)),
                  _layer_spec((1, d), gi), _layer_spec((1, d), go),
                  pl.BlockSpec((None, d, tf), lambda i, j: (layer, 0, j)),
                  pl.BlockSpec((None, d, tf), lambda i, j: (layer, 0, nf + j)),
                  pl.BlockSpec((None, tf, d), lambda i, j: (layer, j, 0))],
        out_specs=pl.BlockSpec((tm, d), lambda i, j: (i, 0)),
        out_shape=jax.ShapeDtypeStruct((m, d), F32),
        scratch_shapes=[pltpu.VMEM((tm, d), BF16)],
        compiler_params=pltpu.CompilerParams(dimension_semantics=("parallel", "arbitrary"),
                                             vmem_limit_bytes=FFN_VMEM_LIMIT),
        name="swiglu_ffn",
    )(x, gains, gains, w_gate_up, w_gate_up, w_down)


def kernel(x, mem, rel_bias, lb_logits, norm_gains, w_in, hg_norm, w_out, w_cq, w_ckv, w_co,
           w_gate_up, w_down):
    batch, seq, d = x.shape
    n_mem = mem.shape[1]
    depth = w_in.shape[0]
    n_norms = norm_gains.shape[1]
    lb_all = _lower_bounds(lb_logits).reshape(depth, 1, HG_WIDTH)
    hg_gain = hg_norm.astype(F32).reshape(depth, 1, HG_WIDTH)
    gains = norm_gains.astype(F32).reshape(depth * n_norms, 1, d)
    rel_bias = rel_bias.astype(F32)
    xf = x.reshape(batch * seq, d).astype(F32)
    memf = mem.reshape(batch * n_mem, d).astype(F32)
    w_in, w_out, w_cq, w_ckv, w_co, w_gate_up, w_down = (
        w.astype(BF16) for w in (w_in, w_out, w_cq, w_ckv, w_co, w_gate_up, w_down))

    for l in range(depth):
        gi = lambda i: l * n_norms + i
        qkv, hproj = _norm_matmul(xf, gains, gi(0), w_in, l, ((3 * ATTN_WIDTH, F32), (4 * HG_WIDTH, BF16)),
                                  tm=1024, tn=1024, name="in_proj")
        attn = _dilated_attention(qkv, rel_bias, batch, seq)
        hg = _hgrn2(hproj, lb_all, hg_gain, l, batch, seq)
        xf = _mix_out(attn, hg, xf, gains, gi(1), w_out, l, tm=512)
        kv, = _norm_matmul(memf, gains, gi(3), w_ckv, l, ((2 * d, BF16),), tm=batch * n_mem, tn=1024,
                           name="mem_kv_proj")
        xf = _cross(xf, gains, gi(2), gi(4), w_cq, kv.reshape(batch, n_mem, 2 * d), w_co, l, seq, tm=512)
        xf = _ffn(xf, gains, gi(5), gi(6), w_gate_up, w_down, l, tm=1024, tf=512)
    return xf.reshape(batch, seq, d).astype(x.dtype)
```

```python
import functools
import math

import numpy as np
import jax
import jax.numpy as jnp
from jax import lax
from jax.experimental import pallas as pl
from jax.experimental.pallas import tpu as pltpu

F32 = jnp.float32
BF16 = jnp.bfloat16

HEAD_DIM = 128
ATTN_HEADS = 8
HG_HEADS = 8
ATTN_WIDTH = ATTN_HEADS * HEAD_DIM
HG_WIDTH = HG_HEADS * HEAD_DIM
Q_BLOCK = 128
DILATED_BRANCHES = ((128, 1), (512, 4), (2048, 16))
ATTN_SPAN = Q_BLOCK * max(d for _, d in DILATED_BRANCHES)
GATHER_STRIDE = 4
MERGE_UNROLL = 8
REL_BUCKETS = 32
REL_MAX_DIST = 2048
CROSS_HEADS = 4
RMS_EPS = 1e-6
NEG_INF = -1e30
HG_CHUNK = 128
HG_LEVELS = 7
HG_STEP_CHUNKS = 4
LOG2_E = 1.4426950408889634
VMEM_LIMIT = 52 * 1024 * 1024
FFN_VMEM_LIMIT = 58 * 1024 * 1024
FFN_ROW_PARTS = 2

PROJ_ROWS = 1024
PROJ_COLS = 1024
RESIDENT_ROWS = 512
FFN_ROWS = 1024
FFN_COLS = 512


def _rms(x, g):
    return x * lax.rsqrt(jnp.mean(x * x, axis=-1, keepdims=True) + RMS_EPS) * g


def _sigmoid(x):
    return 1.0 / (1.0 + jnp.exp(-x))


def _dot(a, b):
    return jnp.dot(a, b, preferred_element_type=F32)


def _dot_nt(a, b):
    return lax.dot_general(a, b, (((1,), (1,)), ((), ())), preferred_element_type=F32)


def _dot_tn(a, b):
    return lax.dot_general(a, b, (((0,), (0,)), ((), ())), preferred_element_type=F32)


def _params(sem):
    return pltpu.CompilerParams(dimension_semantics=sem, vmem_limit_bytes=VMEM_LIMIT)


def _const_spec(shape):
    nd = len(shape)
    return pl.BlockSpec(shape, lambda *_: (0,) * nd, pipeline_mode=pl.Buffered(1))


def _layer_spec(shape, layer):
    nd = len(shape)
    return pl.BlockSpec((None,) + tuple(shape), lambda *_: (layer,) + (0,) * nd,
                        pipeline_mode=pl.Buffered(1))


def _lb_kernel(z_ref, o_ref):
    z = z_ref[...]
    e = jnp.exp(z - jnp.max(z, axis=0, keepdims=True))
    p = e / jnp.sum(e, axis=0, keepdims=True)
    acc = jnp.zeros_like(p[0:1])
    for l in range(z.shape[0]):
        o_ref[l:l + 1, :] = acc
        acc = acc + p[l:l + 1]


def _lower_bounds(lb_logits):
    return pl.pallas_call(
        _lb_kernel, out_shape=jax.ShapeDtypeStruct(lb_logits.shape, F32),
        name="lower_bounds")(lb_logits.astype(F32))


def _norm_matmul_kernel(x_ref, g_ref, w_ref, *refs, splits):
    o_refs, h_scr = refs[:-1], refs[-1]
    j = pl.program_id(1)
    tm = x_ref.shape[0]

    @pl.when(j == 0)
    def _():
        rows = tm // 2
        for r in range(2):
            sl = slice(r * rows, (r + 1) * rows)
            h = _rms(x_ref[sl, :], g_ref[...]).astype(BF16)
            h_scr[sl, :] = h
            o_refs[0][sl, :] = _dot(h, w_ref[...]).astype(o_refs[0].dtype)

    start = 0
    for o_ref, count in zip(o_refs, splits):
        lo, hi = max(start, 1), start + count
        start = hi
        if hi > lo:
            @pl.when((j >= lo) & (j < hi))
            def _(o_ref=o_ref):
                o_ref[...] = _dot(h_scr[...], w_ref[...]).astype(o_ref.dtype)


def _norm_matmul(x, gains, gi, w, layer, outs, tm, tn, name):
    m, d = x.shape
    n = w.shape[2]
    splits = tuple(cols // tn for cols, _ in outs)
    assert sum(splits) * tn == n

    def out_spec(first, count):
        return pl.BlockSpec((tm, tn), lambda i, j: (i, jnp.clip(j - first, 0, count - 1)))

    firsts = [sum(splits[:k]) for k in range(len(splits))]
    return pl.pallas_call(
        functools.partial(_norm_matmul_kernel, splits=splits),
        grid=(m // tm, n // tn),
        in_specs=[pl.BlockSpec((tm, d), lambda i, j: (i, 0)),
                  _layer_spec((1, d), gi),
                  pl.BlockSpec((None, d, tn), lambda i, j: (layer, 0, j))],
        out_specs=[out_spec(first, count) for first, count in zip(firsts, splits)],
        out_shape=[jax.ShapeDtypeStruct((m, cols), dtype) for cols, dtype in outs],
        scratch_shapes=[pltpu.VMEM((tm, d), BF16)],
        compiler_params=_params(("parallel", "arbitrary")),
        name=name,
    )(x, gains, w)


def _bucket_tables():
    qi = np.arange(Q_BLOCK)[:, None]
    kj = np.arange(2 * Q_BLOCK)[None, :]
    m = qi - kj + Q_BLOCK
    max_exact = REL_BUCKETS // 2
    tables = []
    for window, dilation in DILATED_BRANCHES:
        dist = np.maximum(m, 0) * dilation
        d_f = np.maximum(dist, 1).astype(np.float32)
        large = max_exact + (np.log(d_f / np.float32(max_exact))
                             / np.float32(math.log(REL_MAX_DIST / max_exact))
                             * (REL_BUCKETS - max_exact)).astype(np.int32)
        large = np.minimum(large, REL_BUCKETS - 1)
        bucket = np.where(dist < max_exact, dist, large)
        tables.append(np.where((m >= 0) & (m <= window // dilation), bucket, -1))
    return np.stack(tables).astype(np.int32)


def _dilated_kernel(rb_ref, bucket_ref, q_ref, k_ref, v_ref, o_ref,
                    bias_scr, k_scr, v_scr, acc_scr, m_scr, l_scr,
                    qp_scr, kp_scr, vp_scr, accp_scr, mp_scr, lp_scr):
    h = pl.program_id(1)
    j = pl.program_id(2)
    span = ATTN_SPAN
    group = span // GATHER_STRIDE
    cur_half = (j % 2) * span
    prev_half = span - cur_half

    @pl.when(j == 0)
    def _():
        zeros = jnp.zeros((span, HEAD_DIM), F32)
        k_scr[...] = zeros
        v_scr[...] = zeros
        kp_scr[span:2 * span, :] = zeros
        vp_scr[span:2 * span, :] = zeros
        for g in range(len(DILATED_BRANCHES)):
            bucket = bucket_ref[g]
            acc = jnp.full(bucket.shape, NEG_INF, F32)
            for b in range(REL_BUCKETS):
                acc = jnp.where(bucket == b, rb_ref[b, h], acc)
            bias_scr[g] = acc * LOG2_E

    for c in range(GATHER_STRIDE):
        rows = pl.ds(c, group, stride=GATHER_STRIDE)
        dst = pl.ds(pl.multiple_of(cur_half + c * group, group), group)
        qp_scr[c * group:(c + 1) * group, :] = q_ref[rows, :]
        kp_scr[dst, :] = k_ref[rows, :]
        vp_scr[dst, :] = v_ref[rows, :]

    col = lax.broadcasted_iota(jnp.int32, (Q_BLOCK, 2 * Q_BLOCK), 1)
    before_start = (j == 0) & (col < Q_BLOCK)
    ones = jnp.ones((2 * Q_BLOCK, HEAD_DIM), BF16)
    scale = HEAD_DIM ** -0.5 * LOG2_E

    def strided(start, size, stride):
        return pl.ds(start, size, stride=stride) if stride > 1 else pl.ds(start, size)

    def attend(g, first_block, q, k, v):
        s = _dot_nt((q * scale).astype(BF16), k.astype(BF16)) + bias_scr[g]
        if first_block:
            s = jnp.where(before_start, NEG_INF, s)
        m = jnp.max(s, axis=-1, keepdims=True)
        p = jnp.exp2(s - m).astype(BF16)
        pv = _dot(p, jnp.concatenate([v.astype(BF16), ones], axis=1))
        return pv[:, :HEAD_DIM], pv[:, HEAD_DIM:], jnp.broadcast_to(m, (Q_BLOCK, HEAD_DIM))

    for g, (_, d) in enumerate(DILATED_BRANCHES):
        for i in range(span // Q_BLOCK):
            r, n = i % d, i // d
            if d <= GATHER_STRIDE:
                start = n * Q_BLOCK * d + r
                rows_q = strided(start, Q_BLOCK, d)
                if n == 0:
                    prev = strided(span + start - Q_BLOCK * d, Q_BLOCK, d)
                    k = jnp.concatenate([k_scr[prev, :], k_ref[rows_q, :]], axis=0)
                    v = jnp.concatenate([v_scr[prev, :], v_ref[rows_q, :]], axis=0)
                else:
                    rows_kv = strided(start - Q_BLOCK * d, 2 * Q_BLOCK, d)
                    k, v = k_ref[rows_kv, :], v_ref[rows_kv, :]
                acc, l, m = attend(g, n == 0, q_ref[rows_q, :], k, v)
                acc_scr[g, rows_q, :] = acc
                l_scr[g, rows_q, :] = l
                m_scr[g, rows_q, :] = m
            else:
                inner = d // GATHER_STRIDE
                start = (r % GATHER_STRIDE) * group + n * Q_BLOCK * inner + r // GATHER_STRIDE
                rows_q = strided(start, Q_BLOCK, inner)
                if n == 0:
                    prev = strided(prev_half + start + group - Q_BLOCK * inner, Q_BLOCK, inner)
                else:
                    prev = strided(cur_half + start - Q_BLOCK * inner, Q_BLOCK, inner)
                cur = strided(cur_half + start, Q_BLOCK, inner)
                acc, l, m = attend(g, n == 0, qp_scr[rows_q, :],
                                   jnp.concatenate([kp_scr[prev, :], kp_scr[cur, :]], axis=0),
                                   jnp.concatenate([vp_scr[prev, :], vp_scr[cur, :]], axis=0))
                accp_scr[rows_q, :] = acc
                lp_scr[rows_q, :] = l
                mp_scr[rows_q, :] = m
        if d > GATHER_STRIDE:
            for c in range(GATHER_STRIDE):
                rows = pl.ds(c, group, stride=GATHER_STRIDE)
                acc_scr[g, rows, :] = accp_scr[c * group:(c + 1) * group, :]
                l_scr[g, rows, :] = lp_scr[c * group:(c + 1) * group, :]
                m_scr[g, rows, :] = mp_scr[c * group:(c + 1) * group, :]

    k_scr[...] = k_ref[...]
    v_scr[...] = v_ref[...]

    def merge(t, carry):
        rows = pl.ds(pl.multiple_of(t * Q_BLOCK, Q_BLOCK), Q_BLOCK)
        ms = [m_scr[g, rows, :] for g in range(len(DILATED_BRANCHES))]
        top = jnp.maximum(jnp.maximum(ms[0], ms[1]), ms[2])
        num = jnp.zeros((Q_BLOCK, HEAD_DIM), F32)
        den = jnp.zeros((Q_BLOCK, HEAD_DIM), F32)
        for g in range(len(DILATED_BRANCHES)):
            e = jnp.exp2(ms[g] - top)
            num = num + e * acc_scr[g, rows, :]
            den = den + e * l_scr[g, rows, :]
        o_ref[rows, :] = (num / den).astype(o_ref.dtype)
        return carry

    lax.fori_loop(0, span // Q_BLOCK, merge, 0, unroll=MERGE_UNROLL)


def _dilated_attention(qkv, rel_bias, batch, seq):
    span = ATTN_SPAN
    nspan = seq // span
    nbr = len(DILATED_BRANCHES)
    buckets = jnp.asarray(_bucket_tables())

    def part(p):
        return pl.BlockSpec((span, HEAD_DIM), lambda b, h, j: (b * nspan + j, p * ATTN_HEADS + h))

    stat = pltpu.VMEM((nbr, span, HEAD_DIM), F32)
    one_span = pltpu.VMEM((span, HEAD_DIM), F32)
    two_spans = pltpu.VMEM((2 * span, HEAD_DIM), F32)
    return pl.pallas_call(
        _dilated_kernel,
        grid=(batch, ATTN_HEADS, nspan),
        in_specs=[pl.BlockSpec(memory_space=pltpu.SMEM),
                  pl.BlockSpec(buckets.shape, lambda b, h, j: (0, 0, 0)),
                  part(0), part(1), part(2)],
        out_specs=pl.BlockSpec((span, HEAD_DIM), lambda b, h, j: (b * nspan + j, h)),
        out_shape=jax.ShapeDtypeStruct((batch * seq, ATTN_WIDTH), BF16),
        scratch_shapes=[pltpu.VMEM((nbr, Q_BLOCK, 2 * Q_BLOCK), F32),
                        one_span, one_span, stat, stat, stat,
                        one_span, two_spans, two_spans, one_span, one_span, one_span],
        compiler_params=_params(("arbitrary", "arbitrary", "arbitrary")),
        name="dilated_attention",
    )(rel_bias, buckets, qkv, qkv, qkv)


def _hg_tables():
    c = HG_CHUNK
    t = np.arange(c)[:, None]
    u = np.arange(c)[None, :]
    mats = []
    for j in range(1, HG_LEVELS):
        s = 1 << j
        upper = (t & s) != 0
        q_side = (u >= (t & ~(s - 1))) & (u <= t)
        k_side = (u > t) & (u <= (t | (s - 1)))
        mats.append(np.where(upper, q_side, k_side))
    mats.append(u <= t)
    mats.append(u > t)
    stack = np.concatenate(mats, axis=0).astype(np.float32)
    level = np.where(u < t, np.floor(np.log2(np.maximum(t ^ u, 1))).astype(np.int32), -1)
    level = np.where(u == t, HG_LEVELS, level).astype(np.int32)
    return stack, level


def _hgrn_kernel(stack_ref, level_ref, lb_ref, gain_ref, fz_ref, iv_ref, qz_ref, gz_ref, o_ref, st_scr):
    @pl.when(pl.program_id(1) == 0)
    def _():
        st_scr[...] = jnp.zeros_like(st_scr)

    c = HG_CHUNK
    pairs = [slice(2 * p * HEAD_DIM, 2 * (p + 1) * HEAD_DIM) for p in range(HG_HEADS // 2)]
    halves = (slice(0, HEAD_DIM), slice(HEAD_DIM, 2 * HEAD_DIM))
    units = [(slice(ci * c, (ci + 1) * c), sl) for ci in range(HG_STEP_CHUNKS) for sl in pairs]
    level = level_ref[...]

    f, kk, qq, ex = [], [], [], []
    for rows, sl in units:
        lb = lb_ref[:, sl]
        f_u = lb + (1.0 - lb) * _sigmoid(fz_ref[rows, sl].astype(F32))
        lf = jnp.log(f_u) * LOG2_E
        hi = lf.astype(BF16)
        lo = (lf - hi.astype(F32)).astype(BF16)
        ex.append(_dot(stack_ref[...], jnp.concatenate([hi, lo], axis=0)))
        qz = qz_ref[rows, sl].astype(F32)
        f.append(f_u)
        kk.append((1.0 - f_u).astype(BF16))
        qq.append((qz * _sigmoid(qz)).astype(BF16))

    odd = (lax.broadcasted_iota(jnp.int32, (c, 2 * HEAD_DIM), 0) & 1) != 0
    a = [[jnp.zeros((c, c), F32) for _ in halves] for _ in units]
    for j in range(HG_LEVELS + 1):
        for u in range(len(units)):
            if j == 0:
                qy, ky = qq[u] * jnp.where(odd, f[u], 1.0).astype(BF16), kk[u]
            elif j < HG_LEVELS:
                y = jnp.exp2(ex[u][(j - 1) * c:j * c]).astype(BF16)
                qy, ky = qq[u] * y, kk[u] * y
            else:
                qy, ky = qq[u], kk[u]
            for i, half in enumerate(halves):
                a[u][i] = jnp.where(level == j, _dot_nt(qy[:, half], ky[:, half]), a[u][i])

    for u, (rows, sl) in enumerate(units):
        p = u % len(pairs)
        eb = jnp.exp2(ex[u][(HG_LEVELS - 1) * c:HG_LEVELS * c])
        er = jnp.exp2(ex[u][HG_LEVELS * c:(HG_LEVELS + 1) * c])
        qe = qq[u] * eb.astype(BF16)
        ke = kk[u] * er.astype(BF16)
        vv = iv_ref[rows, sl].astype(BF16)
        outs = []
        for i, half in enumerate(halves):
            h = 2 * p + i
            st = st_scr[h]
            o = _dot_nt(qe[:, half], st.astype(BF16)) + _dot(a[u][i].astype(BF16), vv[:, half])
            st_scr[h] = st * eb[c - 1:c, half] + _dot_tn(vv[:, half], ke[:, half])
            outs.append(o * lax.rsqrt(jnp.mean(o * o, axis=-1, keepdims=True) + RMS_EPS))
        gz = gz_ref[rows, sl].astype(F32)
        o = jnp.concatenate(outs, axis=1) * gain_ref[:, sl]
        o_ref[rows, sl] = (o * (gz * _sigmoid(gz))).astype(o_ref.dtype)


def _hgrn2(proj, lb, gain, layer, batch, seq):
    stack, level = _hg_tables()
    stack2 = np.concatenate([stack, stack], axis=1)
    c = HG_CHUNK * HG_STEP_CHUNKS
    nchunk = seq // c

    def part(g):
        return pl.BlockSpec((c, HG_WIDTH), lambda b, n: (b * nchunk + n, g))

    return pl.pallas_call(
        _hgrn_kernel,
        grid=(batch, nchunk),
        in_specs=[_const_spec(stack2.shape), _const_spec(level.shape),
                  _layer_spec((1, HG_WIDTH), layer), _layer_spec((1, HG_WIDTH), layer),
                  part(0), part(1), part(2), part(3)],
        out_specs=pl.BlockSpec((c, HG_WIDTH), lambda b, n: (b * nchunk + n, 0)),
        out_shape=jax.ShapeDtypeStruct((batch * seq, HG_WIDTH), BF16),
        scratch_shapes=[pltpu.VMEM((HG_HEADS, HEAD_DIM, HEAD_DIM), F32)],
        compiler_params=_params(("parallel", "arbitrary")),
        name="hgrn2",
    )(jnp.asarray(stack2, BF16), jnp.asarray(level), lb, gain, proj, proj, proj, proj)


def _mix_out_kernel(attn_ref, hg_ref, x_ref, g_ref, w_ref, out_ref):
    y = _dot(attn_ref[...], w_ref[:ATTN_WIDTH, :]) + _dot(hg_ref[...], w_ref[ATTN_WIDTH:, :])
    out_ref[...] = x_ref[...] + _rms(y, g_ref[...])


def _mix_out(attn, hg, x, gains, gi, w, layer, tm):
    m, d = x.shape
    row = lambda width: pl.BlockSpec((tm, width), lambda i: (i, 0))
    return pl.pallas_call(
        _mix_out_kernel,
        grid=(m // tm,),
        in_specs=[row(ATTN_WIDTH), row(HG_WIDTH), row(d), _layer_spec((1, d), gi),
                  _layer_spec(w.shape[1:], layer)],
        out_specs=row(d),
        out_shape=jax.ShapeDtypeStruct((m, d), F32),
        compiler_params=_params(("parallel",)),
        name="mix_out",
    )(attn, hg, x, gains, w)


def _cross_kernel(x_ref, gq_ref, go_ref, wq_ref, kv_ref, wo_ref, out_ref, co_scr):
    x = x_ref[...]
    d = x.shape[-1]
    hd = d // CROSS_HEADS
    hc = _rms(x, gq_ref[...]).astype(BF16)
    cq = _dot(hc, wq_ref[...]).astype(BF16)
    scale = hd ** -0.5 * LOG2_E
    for h in range(CROSS_HEADS):
        sl = slice(h * hd, (h + 1) * hd)
        s = _dot_nt(cq[:, sl], kv_ref[:, sl]) * scale
        p = jnp.exp2(s - jnp.max(s, axis=-1, keepdims=True))
        p = p * (1.0 / jnp.sum(p, axis=-1, keepdims=True))
        co_scr[:, sl] = _dot(p.astype(BF16), kv_ref[:, d + h * hd:d + (h + 1) * hd]).astype(BF16)
    y = _dot(co_scr[...], wo_ref[...])
    out_ref[...] = x + _rms(y, go_ref[...])


def _cross(x, gains, gq, go, wq, kv, wo, layer, seq, tm):
    m, d = x.shape
    n_mem = kv.shape[1]
    per_batch = seq // tm
    return pl.pallas_call(
        _cross_kernel,
        grid=(m // tm,),
        in_specs=[pl.BlockSpec((tm, d), lambda i: (i, 0)),
                  _layer_spec((1, d), gq), _layer_spec((1, d), go), _layer_spec(wq.shape[1:], layer),
                  pl.BlockSpec((None, n_mem, 2 * d), lambda i: (i // per_batch, 0, 0)),
                  _layer_spec(wo.shape[1:], layer)],
        out_specs=pl.BlockSpec((tm, d), lambda i: (i, 0)),
        out_shape=jax.ShapeDtypeStruct((m, d), F32),
        scratch_shapes=[pltpu.VMEM((tm, d), BF16)],
        compiler_params=_params(("parallel",)),
        name="cross_attention",
    )(x, gains, gains, wq, kv, wo)


def _ffn_kernel(x_ref, gi_ref, go_ref, wg_ref, wu_ref, wd_ref, out_ref, h_scr):
    j = pl.program_id(1)
    last = pl.num_programs(1) - 1
    rows = x_ref.shape[0] // FFN_ROW_PARTS

    def step(first, final):
        for r in range(FFN_ROW_PARTS):
            sl = slice(r * rows, (r + 1) * rows)
            if first:
                h = _rms(x_ref[sl, :], gi_ref[...]).astype(BF16)
                h_scr[sl, :] = h
            else:
                h = h_scr[sl, :]
            gate = _dot(h, wg_ref[...])
            up = _dot(h, wu_ref[...])
            act = (gate * _sigmoid(gate) * up).astype(BF16)
            acc = _dot(act, wd_ref[...])
            if not first:
                acc = out_ref[sl, :] + acc
            if final:
                acc = x_ref[sl, :] + _rms(acc, go_ref[...])
            out_ref[sl, :] = acc

    pl.when(j == 0)(lambda: step(True, False))
    pl.when((j > 0) & (j < last))(lambda: step(False, False))
    pl.when(j == last)(lambda: step(False, True))


def _ffn(x, gains, gi, go, w_gate_up, w_down, layer, tm, tf):
    m, d = x.shape
    nf = w_down.shape[1] // tf
    return pl.pallas_call(
        _ffn_kernel,
        grid=(m // tm, nf),
        in_specs=[pl.BlockSpec((tm, d), lambda i, j: (i, 0)),
                  _layer_spec((1, d), gi), _layer_spec((1, d), go),
                  pl.BlockSpec((None, d, tf), lambda i, j: (layer, 0, j)),
                  pl.BlockSpec((None, d, tf), lambda i, j: (layer, 0, nf + j)),
                  pl.BlockSpec((None, tf, d), lambda i, j: (layer, j, 0))],
        out_specs=pl.BlockSpec((tm, d), lambda i, j: (i, 0)),
        out_shape=jax.ShapeDtypeStruct((m, d), F32),
        scratch_shapes=[pltpu.VMEM((tm, d), BF16)],
        compiler_params=pltpu.CompilerParams(dimension_semantics=("parallel", "arbitrary"),
                                             vmem_limit_bytes=FFN_VMEM_LIMIT),
        name="swiglu_ffn",
    )(x, gains, gains, w_gate_up, w_gate_up, w_down)


def kernel(x, mem, rel_bias, lb_logits, norm_gains, w_in, hg_norm, w_out, w_cq, w_ckv, w_co,
           w_gate_up, w_down):
    batch, seq, d = x.shape
    n_mem = mem.shape[1]
    depth = w_in.shape[0]
    n_norms = norm_gains.shape[1]
    lb_all = _lower_bounds(lb_logits).reshape(depth, 1, HG_WIDTH)
    hg_gain = hg_norm.astype(F32).reshape(depth, 1, HG_WIDTH)
    gains = norm_gains.astype(F32).reshape(depth * n_norms, 1, d)
    rel_bias = rel_bias.astype(F32)
    xf = x.reshape(batch * seq, d).astype(F32)
    memf = mem.reshape(batch * n_mem, d).astype(F32)
    w_in, w_out, w_cq, w_ckv, w_co, w_gate_up, w_down = (
        w.astype(BF16) for w in (w_in, w_out, w_cq, w_ckv, w_co, w_gate_up, w_down))

    for l in range(depth):
        gi = lambda i: l * n_norms + i
        qkv, hproj = _norm_matmul(xf, gains, gi(0), w_in, l, ((3 * ATTN_WIDTH, F32), (4 * HG_WIDTH, BF16)),
                                  tm=PROJ_ROWS, tn=PROJ_COLS, name="in_proj")
        attn = _dilated_attention(qkv, rel_bias, batch, seq)
        hg = _hgrn2(hproj, lb_all, hg_gain, l, batch, seq)
        xf = _mix_out(attn, hg, xf, gains, gi(1), w_out, l, tm=RESIDENT_ROWS)
        kv, = _norm_matmul(memf, gains, gi(3), w_ckv, l, ((2 * d, BF16),), tm=batch * n_mem, tn=PROJ_COLS,
                           name="mem_kv_proj")
        xf = _cross(xf, gains, gi(2), gi(4), w_cq, kv.reshape(batch, n_mem, 2 * d), w_co, l, seq,
                    tm=RESIDENT_ROWS)
        xf = _ffn(xf, gains, gi(5), gi(6), w_gate_up, w_down, l, tm=FFN_ROWS, tf=FFN_COLS)
    return xf.reshape(batch, seq, d).astype(x.dtype)
```

```python
import functools
import math

import numpy as np
import jax
import jax.numpy as jnp
from jax import lax
from jax.experimental import pallas as pl
from jax.experimental.pallas import tpu as pltpu

F32 = jnp.float32
BF16 = jnp.bfloat16

HEAD_DIM = 128
ATTN_HEADS = 8
HG_HEADS = 8
ATTN_WIDTH = ATTN_HEADS * HEAD_DIM
HG_WIDTH = HG_HEADS * HEAD_DIM
Q_BLOCK = 128
DILATED_BRANCHES = ((128, 1), (512, 4), (2048, 16))
ATTN_SPAN = Q_BLOCK * max(d for _, d in DILATED_BRANCHES)
GATHER_STRIDE = 4
MERGE_UNROLL = 8
REL_BUCKETS = 32
REL_MAX_DIST = 2048
CROSS_HEADS = 4
RMS_EPS = 1e-6
NEG_INF = -1e30
HG_CHUNK = 128
HG_LEVELS = 7
HG_STEP_CHUNKS = 4
LOG2_E = 1.4426950408889634
VMEM_LIMIT = 52 * 1024 * 1024
FFN_VMEM_LIMIT = 58 * 1024 * 1024
FFN_ROW_PARTS = 2

PROJ_ROWS = 1024
PROJ_COLS = 1024
RESIDENT_ROWS = 512
FFN_ROWS = 1024
FFN_COLS = 512


def _rms(x, g):
    return x * lax.rsqrt(jnp.mean(x * x, axis=-1, keepdims=True) + RMS_EPS) * g


def _sigmoid(x):
    return 1.0 / (1.0 + jnp.exp2(x * -LOG2_E))


def _dot(a, b):
    return jnp.dot(a, b, preferred_element_type=F32)


def _dot_nt(a, b):
    return lax.dot_general(a, b, (((1,), (1,)), ((), ())), preferred_element_type=F32)


def _dot_tn(a, b):
    return lax.dot_general(a, b, (((0,), (0,)), ((), ())), preferred_element_type=F32)


def _params(sem):
    return pltpu.CompilerParams(dimension_semantics=sem, vmem_limit_bytes=VMEM_LIMIT)


def _const_spec(shape):
    nd = len(shape)
    return pl.BlockSpec(shape, lambda *_: (0,) * nd, pipeline_mode=pl.Buffered(1))


def _layer_spec(shape, layer):
    nd = len(shape)
    return pl.BlockSpec((None,) + tuple(shape), lambda *_: (layer,) + (0,) * nd,
                        pipeline_mode=pl.Buffered(1))


def _lb_kernel(z_ref, o_ref):
    z = z_ref[...]
    e = jnp.exp(z - jnp.max(z, axis=0, keepdims=True))
    p = e / jnp.sum(e, axis=0, keepdims=True)
    acc = jnp.zeros_like(p[0:1])
    for l in range(z.shape[0]):
        o_ref[l:l + 1, :] = acc
        acc = acc + p[l:l + 1]


def _lower_bounds(lb_logits):
    return pl.pallas_call(
        _lb_kernel, out_shape=jax.ShapeDtypeStruct(lb_logits.shape, F32),
        name="lower_bounds")(lb_logits.astype(F32))


def _norm_matmul_kernel(x_ref, g_ref, w_ref, *refs, splits):
    o_refs, h_scr = refs[:-1], refs[-1]
    j = pl.program_id(1)
    tm = x_ref.shape[0]

    @pl.when(j == 0)
    def _():
        rows = tm // 2
        for r in range(2):
            sl = slice(r * rows, (r + 1) * rows)
            h = _rms(x_ref[sl, :], g_ref[...]).astype(BF16)
            h_scr[sl, :] = h
            o_refs[0][sl, :] = _dot(h, w_ref[...]).astype(o_refs[0].dtype)

    start = 0
    for o_ref, count in zip(o_refs, splits):
        lo, hi = max(start, 1), start + count
        start = hi
        if hi > lo:
            @pl.when((j >= lo) & (j < hi))
            def _(o_ref=o_ref):
                o_ref[...] = _dot(h_scr[...], w_ref[...]).astype(o_ref.dtype)


def _norm_matmul(x, gains, gi, w, layer, outs, tm, tn, name):
    m, d = x.shape
    n = w.shape[2]
    splits = tuple(cols // tn for cols, _ in outs)
    assert sum(splits) * tn == n

    def out_spec(first, count):
        return pl.BlockSpec((tm, tn), lambda i, j: (i, jnp.clip(j - first, 0, count - 1)))

    firsts = [sum(splits[:k]) for k in range(len(splits))]
    return pl.pallas_call(
        functools.partial(_norm_matmul_kernel, splits=splits),
        grid=(m // tm, n // tn),
        in_specs=[pl.BlockSpec((tm, d), lambda i, j: (i, 0)),
                  _layer_spec((1, d), gi),
                  pl.BlockSpec((None, d, tn), lambda i, j: (layer, 0, j))],
        out_specs=[out_spec(first, count) for first, count in zip(firsts, splits)],
        out_shape=[jax.ShapeDtypeStruct((m, cols), dtype) for cols, dtype in outs],
        scratch_shapes=[pltpu.VMEM((tm, d), BF16)],
        compiler_params=_params(("parallel", "arbitrary")),
        name=name,
    )(x, gains, w)


def _bucket_tables():
    qi = np.arange(Q_BLOCK)[:, None]
    kj = np.arange(2 * Q_BLOCK)[None, :]
    m = qi - kj + Q_BLOCK
    max_exact = REL_BUCKETS // 2
    tables = []
    for window, dilation in DILATED_BRANCHES:
        dist = np.maximum(m, 0) * dilation
        d_f = np.maximum(dist, 1).astype(np.float32)
        large = max_exact + (np.log(d_f / np.float32(max_exact))
                             / np.float32(math.log(REL_MAX_DIST / max_exact))
                             * (REL_BUCKETS - max_exact)).astype(np.int32)
        large = np.minimum(large, REL_BUCKETS - 1)
        bucket = np.where(dist < max_exact, dist, large)
        tables.append(np.where((m >= 0) & (m <= window // dilation), bucket, -1))
    return np.stack(tables).astype(np.int32)


def _dilated_kernel(rb_ref, bucket_ref, q_ref, k_ref, v_ref, o_ref,
                    bias_scr, k_scr, v_scr, acc_scr, m_scr, l_scr,
                    qp_scr, kp_scr, vp_scr, accp_scr, mp_scr, lp_scr):
    h = pl.program_id(1)
    j = pl.program_id(2)
    span = ATTN_SPAN
    group = span // GATHER_STRIDE
    cur_half = (j % 2) * span
    prev_half = span - cur_half

    @pl.when(j == 0)
    def _():
        zeros = jnp.zeros((span, HEAD_DIM), F32)
        k_scr[...] = zeros
        v_scr[...] = zeros
        kp_scr[span:2 * span, :] = zeros
        vp_scr[span:2 * span, :] = zeros
        col = lax.broadcasted_iota(jnp.int32, (Q_BLOCK, 2 * Q_BLOCK), 1)
        for g in range(len(DILATED_BRANCHES)):
            bucket = bucket_ref[g]
            acc = jnp.full(bucket.shape, NEG_INF, F32)
            for b in range(REL_BUCKETS):
                acc = jnp.where(bucket == b, rb_ref[b, h], acc)
            bias_scr[2 * g] = acc * LOG2_E
            bias_scr[2 * g + 1] = jnp.where(col < Q_BLOCK, NEG_INF, acc * LOG2_E)

    for c in range(GATHER_STRIDE):
        rows = pl.ds(c, group, stride=GATHER_STRIDE)
        dst = pl.ds(pl.multiple_of(cur_half + c * group, group), group)
        qp_scr[c * group:(c + 1) * group, :] = q_ref[rows, :]
        kp_scr[dst, :] = k_ref[rows, :]
        vp_scr[dst, :] = v_ref[rows, :]

    first_span = jnp.where(j == 0, 1, 0)
    ones = jnp.ones((2 * Q_BLOCK, HEAD_DIM), BF16)
    scale = HEAD_DIM ** -0.5 * LOG2_E

    def strided(start, size, stride):
        return pl.ds(start, size, stride=stride) if stride > 1 else pl.ds(start, size)

    def attend(g, first_block, q, k, v):
        bias = bias_scr[2 * g + first_span] if first_block else bias_scr[2 * g]
        s = _dot_nt((q * scale).astype(BF16), k.astype(BF16)) + bias
        m = jnp.max(s, axis=-1, keepdims=True)
        p = jnp.exp2(s - m).astype(BF16)
        pv = _dot(p, jnp.concatenate([v.astype(BF16), ones], axis=1))
        return pv[:, :HEAD_DIM], pv[:, HEAD_DIM:], jnp.broadcast_to(m, (Q_BLOCK, HEAD_DIM))

    for g, (_, d) in enumerate(DILATED_BRANCHES):
        for i in range(span // Q_BLOCK):
            r, n = i % d, i // d
            if d <= GATHER_STRIDE:
                start = n * Q_BLOCK * d + r
                rows_q = strided(start, Q_BLOCK, d)
                if n == 0:
                    prev = strided(span + start - Q_BLOCK * d, Q_BLOCK, d)
                    k = jnp.concatenate([k_scr[prev, :], k_ref[rows_q, :]], axis=0)
                    v = jnp.concatenate([v_scr[prev, :], v_ref[rows_q, :]], axis=0)
                else:
                    rows_kv = strided(start - Q_BLOCK * d, 2 * Q_BLOCK, d)
                    k, v = k_ref[rows_kv, :], v_ref[rows_kv, :]
                acc, l, m = attend(g, n == 0, q_ref[rows_q, :], k, v)
                acc_scr[g, rows_q, :] = acc
                l_scr[g, rows_q, :] = l
                m_scr[g, rows_q, :] = m
            else:
                inner = d // GATHER_STRIDE
                start = (r % GATHER_STRIDE) * group + n * Q_BLOCK * inner + r // GATHER_STRIDE
                rows_q = strided(start, Q_BLOCK, inner)
                if n == 0:
                    prev = strided(prev_half + start + group - Q_BLOCK * inner, Q_BLOCK, inner)
                else:
                    prev = strided(cur_half + start - Q_BLOCK * inner, Q_BLOCK, inner)
                cur = strided(cur_half + start, Q_BLOCK, inner)
                acc, l, m = attend(g, n == 0, qp_scr[rows_q, :],
                                   jnp.concatenate([kp_scr[prev, :], kp_scr[cur, :]], axis=0),
                                   jnp.concatenate([vp_scr[prev, :], vp_scr[cur, :]], axis=0))
                accp_scr[rows_q, :] = acc
                lp_scr[rows_q, :] = l
                mp_scr[rows_q, :] = m
        if d > GATHER_STRIDE:
            for c in range(GATHER_STRIDE):
                rows = pl.ds(c, group, stride=GATHER_STRIDE)
                acc_scr[g, rows, :] = accp_scr[c * group:(c + 1) * group, :]
                l_scr[g, rows, :] = lp_scr[c * group:(c + 1) * group, :]
                m_scr[g, rows, :] = mp_scr[c * group:(c + 1) * group, :]

    k_scr[...] = k_ref[...]
    v_scr[...] = v_ref[...]

    def merge(t, carry):
        rows = pl.ds(pl.multiple_of(t * Q_BLOCK, Q_BLOCK), Q_BLOCK)
        ms = [m_scr[g, rows, :] for g in range(len(DILATED_BRANCHES))]
        top = jnp.maximum(jnp.maximum(ms[0], ms[1]), ms[2])
        num = jnp.zeros((Q_BLOCK, HEAD_DIM), F32)
        den = jnp.zeros((Q_BLOCK, HEAD_DIM), F32)
        for g in range(len(DILATED_BRANCHES)):
            e = jnp.exp2(ms[g] - top)
            num = num + e * acc_scr[g, rows, :]
            den = den + e * l_scr[g, rows, :]
        o_ref[rows, :] = (num / den).astype(o_ref.dtype)
        return carry

    lax.fori_loop(0, span // Q_BLOCK, merge, 0, unroll=MERGE_UNROLL)


def _dilated_attention(qkv, rel_bias, batch, seq):
    span = ATTN_SPAN
    nspan = seq // span
    nbr = len(DILATED_BRANCHES)
    buckets = jnp.asarray(_bucket_tables())

    def part(p):
        return pl.BlockSpec((span, HEAD_DIM), lambda b, h, j: (b * nspan + j, p * ATTN_HEADS + h))

    stat = pltpu.VMEM((nbr, span, HEAD_DIM), F32)
    one_span = pltpu.VMEM((span, HEAD_DIM), F32)
    two_spans = pltpu.VMEM((2 * span, HEAD_DIM), F32)
    return pl.pallas_call(
        _dilated_kernel,
        grid=(batch, ATTN_HEADS, nspan),
        in_specs=[pl.BlockSpec(memory_space=pltpu.SMEM),
                  pl.BlockSpec(buckets.shape, lambda b, h, j: (0, 0, 0)),
                  part(0), part(1), part(2)],
        out_specs=pl.BlockSpec((span, HEAD_DIM), lambda b, h, j: (b * nspan + j, h)),
        out_shape=jax.ShapeDtypeStruct((batch * seq, ATTN_WIDTH), BF16),
        scratch_shapes=[pltpu.VMEM((2 * nbr, Q_BLOCK, 2 * Q_BLOCK), F32),
                        one_span, one_span, stat, stat, stat,
                        one_span, two_spans, two_spans, one_span, one_span, one_span],
        compiler_params=_params(("arbitrary", "arbitrary", "arbitrary")),
        name="dilated_attention",
    )(rel_bias, buckets, qkv, qkv, qkv)


def _hg_tables():
    c = HG_CHUNK
    t = np.arange(c)[:, None]
    u = np.arange(c)[None, :]
    mats = []
    for j in range(1, HG_LEVELS):
        s = 1 << j
        upper = (t & s) != 0
        q_side = (u >= (t & ~(s - 1))) & (u <= t)
        k_side = (u > t) & (u <= (t | (s - 1)))
        mats.append(np.where(upper, q_side, k_side))
    mats.append(u <= t)
    mats.append(u > t)
    stack = np.concatenate(mats, axis=0).astype(np.float32)
    level = np.where(u < t, np.floor(np.log2(np.maximum(t ^ u, 1))).astype(np.int32), -1)
    level = np.where(u == t, HG_LEVELS, level).astype(np.int32)
    return stack, level


def _hgrn_kernel(stack_ref, level_ref, lb_ref, gain_ref, fz_ref, iv_ref, qz_ref, gz_ref, o_ref, st_scr):
    @pl.when(pl.program_id(1) == 0)
    def _():
        st_scr[...] = jnp.zeros_like(st_scr)

    c = HG_CHUNK
    pairs = [slice(2 * p * HEAD_DIM, 2 * (p + 1) * HEAD_DIM) for p in range(HG_HEADS // 2)]
    halves = (slice(0, HEAD_DIM), slice(HEAD_DIM, 2 * HEAD_DIM))
    units = [(slice(ci * c, (ci + 1) * c), sl) for ci in range(HG_STEP_CHUNKS) for sl in pairs]
    level = level_ref[...]

    f, kk, qq, ex = [], [], [], []
    for rows, sl in units:
        lb = lb_ref[:, sl]
        f_u = lb + (1.0 - lb) * _sigmoid(fz_ref[rows, sl].astype(F32))
        lf = jnp.log(f_u) * LOG2_E
        hi = lf.astype(BF16)
        lo = (lf - hi.astype(F32)).astype(BF16)
        ex.append(_dot(stack_ref[...], jnp.concatenate([hi, lo], axis=0)))
        qz = qz_ref[rows, sl].astype(F32)
        f.append(f_u)
        kk.append((1.0 - f_u).astype(BF16))
        qq.append((qz * _sigmoid(qz)).astype(BF16))

    odd = (lax.broadcasted_iota(jnp.int32, (c, 2 * HEAD_DIM), 0) & 1) != 0
    a = [[jnp.zeros((c, c), F32) for _ in halves] for _ in units]
    for j in range(HG_LEVELS + 1):
        for u in range(len(units)):
            if j == 0:
                qy, ky = qq[u] * jnp.where(odd, f[u], 1.0).astype(BF16), kk[u]
            elif j < HG_LEVELS:
                y = jnp.exp2(ex[u][(j - 1) * c:j * c]).astype(BF16)
                qy, ky = qq[u] * y, kk[u] * y
            else:
                qy, ky = qq[u], kk[u]
            for i, half in enumerate(halves):
                a[u][i] = jnp.where(level == j, _dot_nt(qy[:, half], ky[:, half]), a[u][i])

    for u, (rows, sl) in enumerate(units):
        p = u % len(pairs)
        eb = jnp.exp2(ex[u][(HG_LEVELS - 1) * c:HG_LEVELS * c])
        er = jnp.exp2(ex[u][HG_LEVELS * c:(HG_LEVELS + 1) * c])
        qe = qq[u] * eb.astype(BF16)
        ke = kk[u] * er.astype(BF16)
        vv = iv_ref[rows, sl].astype(BF16)
        outs = []
        for i, half in enumerate(halves):
            h = 2 * p + i
            st = st_scr[h]
            o = _dot_nt(qe[:, half], st.astype(BF16)) + _dot(a[u][i].astype(BF16), vv[:, half])
            st_scr[h] = st * eb[c - 1:c, half] + _dot_tn(vv[:, half], ke[:, half])
            outs.append(o * lax.rsqrt(jnp.mean(o * o, axis=-1, keepdims=True) + RMS_EPS))
        gz = gz_ref[rows, sl].astype(F32)
        o = jnp.concatenate(outs, axis=1) * gain_ref[:, sl]
        o_ref[rows, sl] = (o * (gz * _sigmoid(gz))).astype(o_ref.dtype)


def _hgrn2(proj, lb, gain, layer, batch, seq):
    stack, level = _hg_tables()
    stack2 = np.concatenate([stack, stack], axis=1)
    c = HG_CHUNK * HG_STEP_CHUNKS
    nchunk = seq // c

    def part(g):
        return pl.BlockSpec((c, HG_WIDTH), lambda b, n: (b * nchunk + n, g))

    return pl.pallas_call(
        _hgrn_kernel,
        grid=(batch, nchunk),
        in_specs=[_const_spec(stack2.shape), _const_spec(level.shape),
                  _layer_spec((1, HG_WIDTH), layer), _layer_spec((1, HG_WIDTH), layer),
                  part(0), part(1), part(2), part(3)],
        out_specs=pl.BlockSpec((c, HG_WIDTH), lambda b, n: (b * nchunk + n, 0)),
        out_shape=jax.ShapeDtypeStruct((batch * seq, HG_WIDTH), BF16),
        scratch_shapes=[pltpu.VMEM((HG_HEADS, HEAD_DIM, HEAD_DIM), F32)],
        compiler_params=_params(("parallel", "arbitrary")),
        name="hgrn2",
    )(jnp.asarray(stack2, BF16), jnp.asarray(level), lb, gain, proj, proj, proj, proj)


def _mix_out_kernel(attn_ref, hg_ref, x_ref, g_ref, w_ref, out_ref):
    y = _dot(attn_ref[...], w_ref[:ATTN_WIDTH, :]) + _dot(hg_ref[...], w_ref[ATTN_WIDTH:, :])
    out_ref[...] = x_ref[...] + _rms(y, g_ref[...])


def _mix_out(attn, hg, x, gains, gi, w, layer, tm):
    m, d = x.shape
    row = lambda width: pl.BlockSpec((tm, width), lambda i: (i, 0))
    return pl.pallas_call(
        _mix_out_kernel,
        grid=(m // tm,),
        in_specs=[row(ATTN_WIDTH), row(HG_WIDTH), row(d), _layer_spec((1, d), gi),
                  _layer_spec(w.shape[1:], layer)],
        out_specs=row(d),
        out_shape=jax.ShapeDtypeStruct((m, d), F32),
        compiler_params=_params(("parallel",)),
        name="mix_out",
    )(attn, hg, x, gains, w)


def _cross_kernel(x_ref, gq_ref, go_ref, wq_ref, kv_ref, wo_ref, out_ref, co_scr):
    x = x_ref[...]
    d = x.shape[-1]
    hd = d // CROSS_HEADS
    hc = _rms(x, gq_ref[...]).astype(BF16)
    cq = _dot(hc, wq_ref[...]).astype(BF16)
    scale = hd ** -0.5 * LOG2_E
    for h in range(CROSS_HEADS):
        sl = slice(h * hd, (h + 1) * hd)
        s = _dot_nt(cq[:, sl], kv_ref[:, sl]) * scale
        p = jnp.exp2(s - jnp.max(s, axis=-1, keepdims=True))
        p = p * (1.0 / jnp.sum(p, axis=-1, keepdims=True))
        co_scr[:, sl] = _dot(p.astype(BF16), kv_ref[:, d + h * hd:d + (h + 1) * hd]).astype(BF16)
    y = _dot(co_scr[...], wo_ref[...])
    out_ref[...] = x + _rms(y, go_ref[...])


def _cross(x, gains, gq, go, wq, kv, wo, layer, seq, tm):
    m, d = x.shape
    n_mem = kv.shape[1]
    per_batch = seq // tm
    return pl.pallas_call(
        _cross_kernel,
        grid=(m // tm,),
        in_specs=[pl.BlockSpec((tm, d), lambda i: (i, 0)),
                  _layer_spec((1, d), gq), _layer_spec((1, d), go), _layer_spec(wq.shape[1:], layer),
                  pl.BlockSpec((None, n_mem, 2 * d), lambda i: (i // per_batch, 0, 0)),
                  _layer_spec(wo.shape[1:], layer)],
        out_specs=pl.BlockSpec((tm, d), lambda i: (i, 0)),
        out_shape=jax.ShapeDtypeStruct((m, d), F32),
        scratch_shapes=[pltpu.VMEM((tm, d), BF16)],
        compiler_params=_params(("parallel",)),
        name="cross_attention",
    )(x, gains, gains, wq, kv, wo)


def _ffn_kernel(x_ref, gi_ref, go_ref, wg_ref, wu_ref, wd_ref, out_ref, h_scr):
    j = pl.program_id(1)
    last = pl.num_programs(1) - 1
    rows = x_ref.shape[0] // FFN_ROW_PARTS

    def step(first, final):
        for r in range(FFN_ROW_PARTS):
            sl = slice(r * rows, (r + 1) * rows)
            if first:
                h = _rms(x_ref[sl, :], gi_ref[...]).astype(BF16)
                h_scr[sl, :] = h
            else:
                h = h_scr[sl, :]
            gate = _dot(h, wg_ref[...])
            up = _dot(h, wu_ref[...])
            act = (gate * _sigmoid(gate) * up).astype(BF16)
            acc = _dot(act, wd_ref[...])
            if not first:
                acc = out_ref[sl, :] + acc
            if final:
                acc = x_ref[sl, :] + _rms(acc, go_ref[...])
            out_ref[sl, :] = acc

    pl.when(j == 0)(lambda: step(True, False))
    pl.when((j > 0) & (j < last))(lambda: step(False, False))
    pl.when(j == last)(lambda: step(False, True))


def _ffn(x, gains, gi, go, w_gate_up, w_down, layer, tm, tf):
    m, d = x.shape
    nf = w_down.shape[1] // tf
    return pl.pallas_call(
        _ffn_kernel,
        grid=(m // tm, nf),
        in_specs=[pl.BlockSpec((tm, d), lambda i, j: (i, 0)),
                  _layer_spec((1, d), gi), _layer_spec((1, d), go),
                  pl.BlockSpec((None, d, tf), lambda i, j: (layer, 0, j)),
                  pl.BlockSpec((None, d, tf), lambda i, j: (layer, 0, nf + j)),
                  pl.BlockSpec((None, tf, d), lambda i, j: (layer, j, 0))],
        out_specs=pl.BlockSpec((tm, d), lambda i, j: (i, 0)),
        out_shape=jax.ShapeDtypeStruct((m, d), F32),
        scratch_shapes=[pltpu.VMEM((tm, d), BF16)],
        compiler_params=pltpu.CompilerParams(dimension_semantics=("parallel", "arbitrary"),
                                             vmem_limit_bytes=FFN_VMEM_LIMIT),
        name="swiglu_ffn",
    )(x, gains, gains, w_gate_up, w_gate_up, w_down)


def kernel(x, mem, rel_bias, lb_logits, norm_gains, w_in, hg_norm, w_out, w_cq, w_ckv, w_co,
           w_gate_up, w_down):
    batch, seq, d = x.shape
    n_mem = mem.shape[1]
    depth = w_in.shape[0]
    n_norms = norm_gains.shape[1]
    lb_all = _lower_bounds(lb_logits).reshape(depth, 1, HG_WIDTH)
    hg_gain = hg_norm.astype(F32).reshape(depth, 1, HG_WIDTH)
    gains = norm_gains.astype(F32).reshape(depth * n_norms, 1, d)
    rel_bias = rel_bias.astype(F32)
    xf = x.reshape(batch * seq, d).astype(F32)
    memf = mem.reshape(batch * n_mem, d).astype(F32)
    w_in, w_out, w_cq, w_ckv, w_co, w_gate_up, w_down = (
        w.astype(BF16) for w in (w_in, w_out, w_cq, w_ckv, w_co, w_gate_up, w_down))

    for l in range(depth):
        gi = lambda i: l * n_norms + i
        qkv, hproj = _norm_matmul(xf, gains, gi(0), w_in, l, ((3 * ATTN_WIDTH, F32), (4 * HG_WIDTH, BF16)),
                                  tm=PROJ_ROWS, tn=PROJ_COLS, name="in_proj")
        attn = _dilated_attention(qkv, rel_bias, batch, seq)
        hg = _hgrn2(hproj, lb_all, hg_gain, l, batch, seq)
        xf = _mix_out(attn, hg, xf, gains, gi(1), w_out, l, tm=RESIDENT_ROWS)
        kv, = _norm_matmul(memf, gains, gi(3), w_ckv, l, ((2 * d, BF16),), tm=batch * n_mem, tn=PROJ_COLS,
                           name="mem_kv_proj")
        xf = _cross(xf, gains, gi(2), gi(4), w_cq, kv.reshape(batch, n_mem, 2 * d), w_co, l, seq,
                    tm=RESIDENT_ROWS)
        xf = _ffn(xf, gains, gi(5), gi(6), w_gate_up, w_down, l, tm=FFN_ROWS, tf=FFN_COLS)
    return xf.reshape(batch, seq, d).astype(x.dtype)
```

```python
import functools
import math

import numpy as np
import jax
import jax.numpy as jnp
from jax import lax
from jax.experimental import pallas as pl
from jax.experimental.pallas import tpu as pltpu

F32 = jnp.float32
BF16 = jnp.bfloat16

HEAD_DIM = 128
ATTN_HEADS = 8
HG_HEADS = 8
ATTN_WIDTH = ATTN_HEADS * HEAD_DIM
HG_WIDTH = HG_HEADS * HEAD_DIM
Q_BLOCK = 128
DILATED_BRANCHES = ((128, 1), (512, 4), (2048, 16))
ATTN_SPAN = Q_BLOCK * max(d for _, d in DILATED_BRANCHES)
GATHER_STRIDE = 4
MERGE_UNROLL = 8
REL_BUCKETS = 32
REL_MAX_DIST = 2048
CROSS_HEADS = 4
RMS_EPS = 1e-6
NEG_INF = -1e30
HG_CHUNK = 128
HG_LEVELS = 7
HG_STEP_CHUNKS = 4
LOG2_E = 1.4426950408889634
VMEM_LIMIT = 52 * 1024 * 1024
FFN_VMEM_LIMIT = 58 * 1024 * 1024
FFN_ROW_PARTS = 2

PROJ_ROWS = 1024
PROJ_COLS = 1024
RESIDENT_ROWS = 512
FFN_ROWS = 1024
FFN_COLS = 512


def _rms(x, g):
    return x * lax.rsqrt(jnp.mean(x * x, axis=-1, keepdims=True) + RMS_EPS) * g


def _sigmoid(x):
    return 1.0 / (1.0 + jnp.exp2(x * -LOG2_E))


def _dot(a, b):
    return jnp.dot(a, b, preferred_element_type=F32)


def _dot_nt(a, b):
    return lax.dot_general(a, b, (((1,), (1,)), ((), ())), preferred_element_type=F32)


def _dot_tn(a, b):
    return lax.dot_general(a, b, (((0,), (0,)), ((), ())), preferred_element_type=F32)


def _params(sem):
    return pltpu.CompilerParams(dimension_semantics=sem, vmem_limit_bytes=VMEM_LIMIT)


def _const_spec(shape):
    nd = len(shape)
    return pl.BlockSpec(shape, lambda *_: (0,) * nd, pipeline_mode=pl.Buffered(1))


def _layer_spec(shape, layer):
    nd = len(shape)
    return pl.BlockSpec((None,) + tuple(shape), lambda *_: (layer,) + (0,) * nd,
                        pipeline_mode=pl.Buffered(1))


def _lb_kernel(z_ref, o_ref):
    z = z_ref[...]
    e = jnp.exp(z - jnp.max(z, axis=0, keepdims=True))
    p = e / jnp.sum(e, axis=0, keepdims=True)
    acc = jnp.zeros_like(p[0:1])
    for l in range(z.shape[0]):
        o_ref[l:l + 1, :] = acc
        acc = acc + p[l:l + 1]


def _lower_bounds(lb_logits):
    return pl.pallas_call(
        _lb_kernel, out_shape=jax.ShapeDtypeStruct(lb_logits.shape, F32),
        name="lower_bounds")(lb_logits.astype(F32))


def _norm_matmul_kernel(x_ref, g_ref, w_ref, *refs, splits):
    o_refs, h_scr = refs[:-1], refs[-1]
    j = pl.program_id(1)
    tm = x_ref.shape[0]

    @pl.when(j == 0)
    def _():
        rows = tm // 2
        for r in range(2):
            sl = slice(r * rows, (r + 1) * rows)
            h = _rms(x_ref[sl, :], g_ref[...]).astype(BF16)
            h_scr[sl, :] = h
            o_refs[0][sl, :] = _dot(h, w_ref[...].astype(BF16)).astype(o_refs[0].dtype)

    start = 0
    for o_ref, count in zip(o_refs, splits):
        lo, hi = max(start, 1), start + count
        start = hi
        if hi > lo:
            @pl.when((j >= lo) & (j < hi))
            def _(o_ref=o_ref):
                o_ref[...] = _dot(h_scr[...], w_ref[...].astype(BF16)).astype(o_ref.dtype)


def _norm_matmul(x, gains, gi, w, layer, outs, tm, tn, name):
    m, d = x.shape
    n = w.shape[2]
    splits = tuple(cols // tn for cols, _ in outs)
    assert sum(splits) * tn == n

    def out_spec(first, count):
        return pl.BlockSpec((tm, tn), lambda i, j: (i, jnp.clip(j - first, 0, count - 1)))

    firsts = [sum(splits[:k]) for k in range(len(splits))]
    return pl.pallas_call(
        functools.partial(_norm_matmul_kernel, splits=splits),
        grid=(m // tm, n // tn),
        in_specs=[pl.BlockSpec((tm, d), lambda i, j: (i, 0)),
                  _layer_spec((1, d), gi),
                  pl.BlockSpec((None, d, tn), lambda i, j: (layer, 0, j))],
        out_specs=[out_spec(first, count) for first, count in zip(firsts, splits)],
        out_shape=[jax.ShapeDtypeStruct((m, cols), dtype) for cols, dtype in outs],
        scratch_shapes=[pltpu.VMEM((tm, d), BF16)],
        compiler_params=_params(("parallel", "arbitrary")),
        name=name,
    )(x, gains, w)


def _bucket_tables():
    qi = np.arange(Q_BLOCK)[:, None]
    kj = np.arange(2 * Q_BLOCK)[None, :]
    m = qi - kj + Q_BLOCK
    max_exact = REL_BUCKETS // 2
    tables = []
    for window, dilation in DILATED_BRANCHES:
        dist = np.maximum(m, 0) * dilation
        d_f = np.maximum(dist, 1).astype(np.float32)
        large = max_exact + (np.log(d_f / np.float32(max_exact))
                             / np.float32(math.log(REL_MAX_DIST / max_exact))
                             * (REL_BUCKETS - max_exact)).astype(np.int32)
        large = np.minimum(large, REL_BUCKETS - 1)
        bucket = np.where(dist < max_exact, dist, large)
        tables.append(np.where((m >= 0) & (m <= window // dilation), bucket, -1))
    return np.stack(tables).astype(np.int32)


def _dilated_kernel(rb_ref, bucket_ref, q_ref, k_ref, v_ref, o_ref,
                    bias_scr, k_scr, v_scr, acc_scr, m_scr, l_scr,
                    qp_scr, kp_scr, vp_scr, accp_scr, mp_scr, lp_scr):
    h = pl.program_id(1)
    j = pl.program_id(2)
    span = ATTN_SPAN
    group = span // GATHER_STRIDE
    cur_half = (j % 2) * span
    prev_half = span - cur_half

    @pl.when(j == 0)
    def _():
        zeros = jnp.zeros((span, HEAD_DIM), F32)
        k_scr[...] = zeros
        v_scr[...] = zeros
        kp_scr[span:2 * span, :] = zeros
        vp_scr[span:2 * span, :] = zeros
        col = lax.broadcasted_iota(jnp.int32, (Q_BLOCK, 2 * Q_BLOCK), 1)
        for g in range(len(DILATED_BRANCHES)):
            bucket = bucket_ref[g]
            acc = jnp.full(bucket.shape, NEG_INF, F32)
            for b in range(REL_BUCKETS):
                acc = jnp.where(bucket == b, rb_ref[b, h], acc)
            bias_scr[2 * g] = acc * LOG2_E
            bias_scr[2 * g + 1] = jnp.where(col < Q_BLOCK, NEG_INF, acc * LOG2_E)

    for c in range(GATHER_STRIDE):
        rows = pl.ds(c, group, stride=GATHER_STRIDE)
        dst = pl.ds(pl.multiple_of(cur_half + c * group, group), group)
        qp_scr[c * group:(c + 1) * group, :] = q_ref[rows, :]
        kp_scr[dst, :] = k_ref[rows, :]
        vp_scr[dst, :] = v_ref[rows, :]

    first_span = jnp.where(j == 0, 1, 0)
    ones = jnp.ones((2 * Q_BLOCK, HEAD_DIM), BF16)
    scale = HEAD_DIM ** -0.5 * LOG2_E

    def strided(start, size, stride):
        return pl.ds(start, size, stride=stride) if stride > 1 else pl.ds(start, size)

    def attend(g, first_block, q, k, v):
        bias = bias_scr[2 * g + first_span] if first_block else bias_scr[2 * g]
        s = _dot_nt((q * scale).astype(BF16), k.astype(BF16)) + bias
        m = jnp.max(s, axis=-1, keepdims=True)
        p = jnp.exp2(s - m).astype(BF16)
        pv = _dot(p, jnp.concatenate([v.astype(BF16), ones], axis=1))
        return pv[:, :HEAD_DIM], pv[:, HEAD_DIM:], jnp.broadcast_to(m, (Q_BLOCK, HEAD_DIM))

    for g, (_, d) in enumerate(DILATED_BRANCHES):
        for i in range(span // Q_BLOCK):
            r, n = i % d, i // d
            if d <= GATHER_STRIDE:
                start = n * Q_BLOCK * d + r
                rows_q = strided(start, Q_BLOCK, d)
                if n == 0:
                    prev = strided(span + start - Q_BLOCK * d, Q_BLOCK, d)
                    k = jnp.concatenate([k_scr[prev, :], k_ref[rows_q, :]], axis=0)
                    v = jnp.concatenate([v_scr[prev, :], v_ref[rows_q, :]], axis=0)
                else:
                    rows_kv = strided(start - Q_BLOCK * d, 2 * Q_BLOCK, d)
                    k, v = k_ref[rows_kv, :], v_ref[rows_kv, :]
                acc, l, m = attend(g, n == 0, q_ref[rows_q, :], k, v)
                acc_scr[g, rows_q, :] = acc
                l_scr[g, rows_q, :] = l
                m_scr[g, rows_q, :] = m
            else:
                inner = d // GATHER_STRIDE
                start = (r % GATHER_STRIDE) * group + n * Q_BLOCK * inner + r // GATHER_STRIDE
                rows_q = strided(start, Q_BLOCK, inner)
                if n == 0:
                    prev = strided(prev_half + start + group - Q_BLOCK * inner, Q_BLOCK, inner)
                else:
                    prev = strided(cur_half + start - Q_BLOCK * inner, Q_BLOCK, inner)
                cur = strided(cur_half + start, Q_BLOCK, inner)
                acc, l, m = attend(g, n == 0, qp_scr[rows_q, :],
                                   jnp.concatenate([kp_scr[prev, :], kp_scr[cur, :]], axis=0),
                                   jnp.concatenate([vp_scr[prev, :], vp_scr[cur, :]], axis=0))
                accp_scr[rows_q, :] = acc
                lp_scr[rows_q, :] = l
                mp_scr[rows_q, :] = m
        if d > GATHER_STRIDE:
            for c in range(GATHER_STRIDE):
                rows = pl.ds(c, group, stride=GATHER_STRIDE)
                acc_scr[g, rows, :] = accp_scr[c * group:(c + 1) * group, :]
                l_scr[g, rows, :] = lp_scr[c * group:(c + 1) * group, :]
                m_scr[g, rows, :] = mp_scr[c * group:(c + 1) * group, :]

    k_scr[...] = k_ref[...]
    v_scr[...] = v_ref[...]

    def merge(t, carry):
        rows = pl.ds(pl.multiple_of(t * Q_BLOCK, Q_BLOCK), Q_BLOCK)
        ms = [m_scr[g, rows, :] for g in range(len(DILATED_BRANCHES))]
        top = jnp.maximum(jnp.maximum(ms[0], ms[1]), ms[2])
        num = jnp.zeros((Q_BLOCK, HEAD_DIM), F32)
        den = jnp.zeros((Q_BLOCK, HEAD_DIM), F32)
        for g in range(len(DILATED_BRANCHES)):
            e = jnp.exp2(ms[g] - top)
            num = num + e * acc_scr[g, rows, :]
            den = den + e * l_scr[g, rows, :]
        o_ref[rows, :] = (num / den).astype(o_ref.dtype)
        return carry

    lax.fori_loop(0, span // Q_BLOCK, merge, 0, unroll=MERGE_UNROLL)


def _dilated_attention(qkv, rel_bias, batch, seq):
    span = ATTN_SPAN
    nspan = seq // span
    nbr = len(DILATED_BRANCHES)
    buckets = jnp.asarray(_bucket_tables())

    def part(p):
        return pl.BlockSpec((span, HEAD_DIM), lambda b, h, j: (b * nspan + j, p * ATTN_HEADS + h))

    stat = pltpu.VMEM((nbr, span, HEAD_DIM), F32)
    one_span = pltpu.VMEM((span, HEAD_DIM), F32)
    two_spans = pltpu.VMEM((2 * span, HEAD_DIM), F32)
    return pl.pallas_call(
        _dilated_kernel,
        grid=(batch, ATTN_HEADS, nspan),
        in_specs=[pl.BlockSpec(memory_space=pltpu.SMEM),
                  pl.BlockSpec(buckets.shape, lambda b, h, j: (0, 0, 0)),
                  part(0), part(1), part(2)],
        out_specs=pl.BlockSpec((span, HEAD_DIM), lambda b, h, j: (b * nspan + j, h)),
        out_shape=jax.ShapeDtypeStruct((batch * seq, ATTN_WIDTH), BF16),
        scratch_shapes=[pltpu.VMEM((2 * nbr, Q_BLOCK, 2 * Q_BLOCK), F32),
                        one_span, one_span, stat, stat, stat,
                        one_span, two_spans, two_spans, one_span, one_span, one_span],
        compiler_params=_params(("arbitrary", "arbitrary", "arbitrary")),
        name="dilated_attention",
    )(rel_bias, buckets, qkv, qkv, qkv)


def _hg_tables():
    c = HG_CHUNK
    t = np.arange(c)[:, None]
    u = np.arange(c)[None, :]
    mats = []
    for j in range(1, HG_LEVELS):
        s = 1 << j
        upper = (t & s) != 0
        q_side = (u >= (t & ~(s - 1))) & (u <= t)
        k_side = (u > t) & (u <= (t | (s - 1)))
        mats.append(np.where(upper, q_side, k_side))
    mats.append(u <= t)
    mats.append(u > t)
    stack = np.concatenate(mats, axis=0).astype(np.float32)
    level = np.where(u < t, np.floor(np.log2(np.maximum(t ^ u, 1))).astype(np.int32), -1)
    level = np.where(u == t, HG_LEVELS, level).astype(np.int32)
    return stack, level


def _hgrn_kernel(stack_ref, level_ref, lb_ref, gain_ref, fz_ref, iv_ref, qz_ref, gz_ref, o_ref, st_scr):
    @pl.when(pl.program_id(1) == 0)
    def _():
        st_scr[...] = jnp.zeros_like(st_scr)

    c = HG_CHUNK
    pairs = [slice(2 * p * HEAD_DIM, 2 * (p + 1) * HEAD_DIM) for p in range(HG_HEADS // 2)]
    halves = (slice(0, HEAD_DIM), slice(HEAD_DIM, 2 * HEAD_DIM))
    units = [(slice(ci * c, (ci + 1) * c), sl) for ci in range(HG_STEP_CHUNKS) for sl in pairs]
    level = level_ref[...]

    f, kk, qq, ex = [], [], [], []
    for rows, sl in units:
        lb = lb_ref[:, sl]
        f_u = lb + (1.0 - lb) * _sigmoid(fz_ref[rows, sl].astype(F32))
        lf = jnp.log(f_u) * LOG2_E
        hi = lf.astype(BF16)
        lo = (lf - hi.astype(F32)).astype(BF16)
        ex.append(_dot(stack_ref[...], jnp.concatenate([hi, lo], axis=0)))
        qz = qz_ref[rows, sl].astype(F32)
        f.append(f_u)
        kk.append((1.0 - f_u).astype(BF16))
        qq.append((qz * _sigmoid(qz)).astype(BF16))

    odd = (lax.broadcasted_iota(jnp.int32, (c, 2 * HEAD_DIM), 0) & 1) != 0
    a = [[jnp.zeros((c, c), F32) for _ in halves] for _ in units]
    for j in range(HG_LEVELS + 1):
        for u in range(len(units)):
            if j == 0:
                qy, ky = qq[u] * jnp.where(odd, f[u], 1.0).astype(BF16), kk[u]
            elif j < HG_LEVELS:
                y = jnp.exp2(ex[u][(j - 1) * c:j * c]).astype(BF16)
                qy, ky = qq[u] * y, kk[u] * y
            else:
                qy, ky = qq[u], kk[u]
            for i, half in enumerate(halves):
                a[u][i] = jnp.where(level == j, _dot_nt(qy[:, half], ky[:, half]), a[u][i])

    for u, (rows, sl) in enumerate(units):
        p = u % len(pairs)
        eb = jnp.exp2(ex[u][(HG_LEVELS - 1) * c:HG_LEVELS * c])
        er = jnp.exp2(ex[u][HG_LEVELS * c:(HG_LEVELS + 1) * c])
        qe = qq[u] * eb.astype(BF16)
        ke = kk[u] * er.astype(BF16)
        vv = iv_ref[rows, sl].astype(BF16)
        outs = []
        for i, half in enumerate(halves):
            h = 2 * p + i
            st = st_scr[h]
            o = _dot_nt(qe[:, half], st.astype(BF16)) + _dot(a[u][i].astype(BF16), vv[:, half])
            st_scr[h] = st * eb[c - 1:c, half] + _dot_tn(vv[:, half], ke[:, half])
            outs.append(o * lax.rsqrt(jnp.mean(o * o, axis=-1, keepdims=True) + RMS_EPS))
        gz = gz_ref[rows, sl].astype(F32)
        o = jnp.concatenate(outs, axis=1) * gain_ref[:, sl]
        o_ref[rows, sl] = (o * (gz * _sigmoid(gz))).astype(o_ref.dtype)


def _hgrn2(proj, lb, gain, layer, batch, seq):
    stack, level = _hg_tables()
    stack2 = np.concatenate([stack, stack], axis=1)
    c = HG_CHUNK * HG_STEP_CHUNKS
    nchunk = seq // c

    def part(g):
        return pl.BlockSpec((c, HG_WIDTH), lambda b, n: (b * nchunk + n, g))

    return pl.pallas_call(
        _hgrn_kernel,
        grid=(batch, nchunk),
        in_specs=[_const_spec(stack2.shape), _const_spec(level.shape),
                  _layer_spec((1, HG_WIDTH), layer), _layer_spec((1, HG_WIDTH), layer),
                  part(0), part(1), part(2), part(3)],
        out_specs=pl.BlockSpec((c, HG_WIDTH), lambda b, n: (b * nchunk + n, 0)),
        out_shape=jax.ShapeDtypeStruct((batch * seq, HG_WIDTH), BF16),
        scratch_shapes=[pltpu.VMEM((HG_HEADS, HEAD_DIM, HEAD_DIM), F32)],
        compiler_params=_params(("parallel", "arbitrary")),
        name="hgrn2",
    )(jnp.asarray(stack2, BF16), jnp.asarray(level), lb, gain, proj, proj, proj, proj)


def _mix_out_kernel(attn_ref, hg_ref, x_ref, g_ref, w_ref, out_ref, w_scr):
    @pl.when(pl.program_id(0) == 0)
    def _():
        w_scr[...] = w_ref[...].astype(BF16)

    y = _dot(attn_ref[...], w_scr[:ATTN_WIDTH, :]) + _dot(hg_ref[...], w_scr[ATTN_WIDTH:, :])
    out_ref[...] = x_ref[...] + _rms(y, g_ref[...])


def _mix_out(attn, hg, x, gains, gi, w, layer, tm):
    m, d = x.shape
    row = lambda width: pl.BlockSpec((tm, width), lambda i: (i, 0))
    return pl.pallas_call(
        _mix_out_kernel,
        grid=(m // tm,),
        in_specs=[row(ATTN_WIDTH), row(HG_WIDTH), row(d), _layer_spec((1, d), gi),
                  _layer_spec(w.shape[1:], layer)],
        out_specs=row(d),
        out_shape=jax.ShapeDtypeStruct((m, d), F32),
        scratch_shapes=[pltpu.VMEM(w.shape[1:], BF16)],
        compiler_params=_params(("arbitrary",)),
        name="mix_out",
    )(attn, hg, x, gains, w)


def _cross_kernel(x_ref, gq_ref, go_ref, wq_ref, kv_ref, wo_ref, out_ref, co_scr):
    x = x_ref[...]
    d = x.shape[-1]
    hd = d // CROSS_HEADS
    hc = _rms(x, gq_ref[...]).astype(BF16)
    cq = _dot(hc, wq_ref[...]).astype(BF16)
    scale = hd ** -0.5 * LOG2_E
    for h in range(CROSS_HEADS):
        sl = slice(h * hd, (h + 1) * hd)
        s = _dot_nt(cq[:, sl], kv_ref[:, sl]) * scale
        p = jnp.exp2(s - jnp.max(s, axis=-1, keepdims=True))
        p = p * (1.0 / jnp.sum(p, axis=-1, keepdims=True))
        co_scr[:, sl] = _dot(p.astype(BF16), kv_ref[:, d + h * hd:d + (h + 1) * hd]).astype(BF16)
    y = _dot(co_scr[...], wo_ref[...])
    out_ref[...] = x + _rms(y, go_ref[...])


def _cross(x, gains, gq, go, wq, kv, wo, layer, seq, tm):
    m, d = x.shape
    n_mem = kv.shape[1]
    per_batch = seq // tm
    return pl.pallas_call(
        _cross_kernel,
        grid=(m // tm,),
        in_specs=[pl.BlockSpec((tm, d), lambda i: (i, 0)),
                  _layer_spec((1, d), gq), _layer_spec((1, d), go), _layer_spec(wq.shape[1:], layer),
                  pl.BlockSpec((None, n_mem, 2 * d), lambda i: (i // per_batch, 0, 0)),
                  _layer_spec(wo.shape[1:], layer)],
        out_specs=pl.BlockSpec((tm, d), lambda i: (i, 0)),
        out_shape=jax.ShapeDtypeStruct((m, d), F32),
        scratch_shapes=[pltpu.VMEM((tm, d), BF16)],
        compiler_params=_params(("parallel",)),
        name="cross_attention",
    )(x, gains, gains, wq, kv, wo)


def _ffn_kernel(x_ref, gi_ref, go_ref, wg_ref, wu_ref, wd_ref, out_ref, h_scr):
    j = pl.program_id(1)
    last = pl.num_programs(1) - 1
    rows = x_ref.shape[0] // FFN_ROW_PARTS

    def step(first, final):
        for r in range(FFN_ROW_PARTS):
            sl = slice(r * rows, (r + 1) * rows)
            if first:
                h = _rms(x_ref[sl, :], gi_ref[...]).astype(BF16)
                h_scr[sl, :] = h
            else:
                h = h_scr[sl, :]
            gate = _dot(h, wg_ref[...])
            up = _dot(h, wu_ref[...])
            act = (gate * _sigmoid(gate) * up).astype(BF16)
            acc = _dot(act, wd_ref[...])
            if not first:
                acc = out_ref[sl, :] + acc
            if final:
                acc = x_ref[sl, :] + _rms(acc, go_ref[...])
            out_ref[sl, :] = acc

    pl.when(j == 0)(lambda: step(True, False))
    pl.when((j > 0) & (j < last))(lambda: step(False, False))
    pl.when(j == last)(lambda: step(False, True))


def _ffn(x, gains, gi, go, w_gate_up, w_down, layer, tm, tf):
    m, d = x.shape
    nf = w_down.shape[1] // tf
    return pl.pallas_call(
        _ffn_kernel,
        grid=(m // tm, nf),
        in_specs=[pl.BlockSpec((tm, d), lambda i, j: (i, 0)),
                  _layer_spec((1, d), gi), _layer_spec((1, d), go),
                  pl.BlockSpec((None, d, tf), lambda i, j: (layer, 0, j)),
                  pl.BlockSpec((None, d, tf), lambda i, j: (layer, 0, nf + j)),
                  pl.BlockSpec((None, tf, d), lambda i, j: (layer, j, 0))],
        out_specs=pl.BlockSpec((tm, d), lambda i, j: (i, 0)),
        out_shape=jax.ShapeDtypeStruct((m, d), F32),
        scratch_shapes=[pltpu.VMEM((tm, d), BF16)],
        compiler_params=pltpu.CompilerParams(dimension_semantics=("parallel", "arbitrary"),
                                             vmem_limit_bytes=FFN_VMEM_LIMIT),
        name="swiglu_ffn",
    )(x, gains, gains, w_gate_up, w_gate_up, w_down)


def kernel(x, mem, rel_bias, lb_logits, norm_gains, w_in, hg_norm, w_out, w_cq, w_ckv, w_co,
           w_gate_up, w_down):
    batch, seq, d = x.shape
    n_mem = mem.shape[1]
    depth = w_in.shape[0]
    n_norms = norm_gains.shape[1]
    lb_all = _lower_bounds(lb_logits).reshape(depth, 1, HG_WIDTH)
    hg_gain = hg_norm.astype(F32).reshape(depth, 1, HG_WIDTH)
    gains = norm_gains.astype(F32).reshape(depth * n_norms, 1, d)
    rel_bias = rel_bias.astype(F32)
    xf = x.reshape(batch * seq, d).astype(F32)
    memf = mem.reshape(batch * n_mem, d).astype(F32)
    w_in, w_cq, w_co, w_gate_up, w_down = (w.astype(BF16) for w in (w_in, w_cq, w_co, w_gate_up, w_down))
    w_out, w_ckv = w_out.astype(F32), w_ckv.astype(F32)

    for l in range(depth):
        gi = lambda i: l * n_norms + i
        qkv, hproj = _norm_matmul(xf, gains, gi(0), w_in, l, ((3 * ATTN_WIDTH, F32), (4 * HG_WIDTH, BF16)),
                                  tm=PROJ_ROWS, tn=PROJ_COLS, name="in_proj")
        attn = _dilated_attention(qkv, rel_bias, batch, seq)
        hg = _hgrn2(hproj, lb_all, hg_gain, l, batch, seq)
        xf = _mix_out(attn, hg, xf, gains, gi(1), w_out, l, tm=RESIDENT_ROWS)
        kv, = _norm_matmul(memf, gains, gi(3), w_ckv, l, ((2 * d, BF16),), tm=batch * n_mem, tn=PROJ_COLS,
                           name="mem_kv_proj")
        xf = _cross(xf, gains, gi(2), gi(4), w_cq, kv.reshape(batch, n_mem, 2 * d), w_co, l, seq,
                    tm=RESIDENT_ROWS)
        xf = _ffn(xf, gains, gi(5), gi(6), w_gate_up, w_down, l, tm=FFN_ROWS, tf=FFN_COLS)
    return xf.reshape(batch, seq, d).astype(x.dtype)
```
